```python
import math
import jax, jax.numpy as jnp
from jax import lax
import numpy as np

D_MODEL = 4096
BATCH = 4
SEQ = 2048
DEPTH = 4
DEC_BATCH = 32
DEC_SEQ = 4
PAST_LEN = 8192
PAGE_SIZE = 128

N_EVEN = (DEPTH + 1) // 2
N_ODD = DEPTH // 2
EPS = 1e-6
NEG_INF = -1e30
MIX_WIDTH = D_MODEL
A_WIDTH = MIX_WIDTH // 2
A_HEADS = 8
A_HEAD_DIM = A_WIDTH // A_HEADS
CHUNK = 128
B_WIDTH = MIX_WIDTH - A_WIDTH
CONV_WIDTH = 3
EVEN_PROJ = 2 * A_WIDTH + 3 * B_WIDTH
HEAD_DIM = 64
N_HEADS = D_MODEL // HEAD_DIM
N_KV_HEADS = 8
GQA_GROUP = N_HEADS // N_KV_HEADS
WINDOW = 128
ATTN_BLOCK = 128
QKV_PROJ = (N_HEADS + 2 * N_KV_HEADS) * HEAD_DIM
N_BUCKETS = 32
MAX_DISTANCE = 128
D_FF = -(-8 * D_MODEL // (3 * 256)) * 256

kernel_name = "hybrid_sgu_conv_swa_decoder_step"


def rms_norm(x, g):
    xf = x.astype(jnp.float32)
    y = xf * lax.rsqrt(jnp.mean(xf * xf, axis=-1, keepdims=True) + EPS)
    return (y * g.astype(jnp.float32)).astype(x.dtype)


def swiglu(h, wg, wu, wd):
    return (jax.nn.silu(h @ wg) * (h @ wu)) @ wd


def even_inputs(h, w_in):
    proj = h @ w_in
    u, v, xb, gc, gb = jnp.split(
        proj, [A_WIDTH, 2 * A_WIDTH, 2 * A_WIDTH + B_WIDTH, 2 * A_WIDTH + 2 * B_WIDTH], axis=-1)
    u = jax.nn.gelu(u)
    v = jax.nn.gelu(v).reshape(*h.shape[:-1], A_HEADS, A_HEAD_DIM)
    z = gc * xb
    return u, v, z, gb


def spatial_gate(u, v, w_s, b_s):
    L = v.shape[-3]
    w = jnp.tril(w_s[:, :L, :L])
    mixed = jnp.einsum('hij,ncjhd->ncihd', w, v) + b_s[:, :L].T[:, :, None]
    return u * mixed.reshape(u.shape)


def causal_dwconv(z_ext, w):
    return lax.conv_general_dilated(
        z_ext, w[:, None, :].astype(z_ext.dtype), window_strides=(1,), padding='VALID',
        dimension_numbers=('NWC', 'WIO', 'NWC'), feature_group_count=z_ext.shape[-1])


def even_mixer_prompt(h, w_in, w_out, w_s, b_s, conv_w):
    n, s, _ = h.shape
    nc = s // CHUNK
    u, v, z, gb = even_inputs(h, w_in)
    a = spatial_gate(u.reshape(n, nc, CHUNK, A_WIDTH),
                     v.reshape(n, nc, CHUNK, A_HEADS, A_HEAD_DIM), w_s, b_s).reshape(n, s, A_WIDTH)
    z_ext = jnp.pad(z, ((0, 0), (CONV_WIDTH - 1, 0), (0, 0)))
    b = gb * causal_dwconv(z_ext, conv_w)
    y = jnp.concatenate([a, b], axis=-1) @ w_out
    return y, z[:, s - (CONV_WIDTH - 1):]


def even_mixer_sample(h, conv_state, w_in, w_out, w_s, b_s, conv_w):
    u, v, z, gb = even_inputs(h, w_in)
    a = spatial_gate(u[:, None], v[:, None], w_s, b_s)[:, 0]
    z_ext = jnp.concatenate([conv_state.astype(z.dtype), z], axis=1)
    b = gb * causal_dwconv(z_ext, conv_w)
    y = jnp.concatenate([a, b], axis=-1) @ w_out
    return y, z_ext[:, z_ext.shape[1] - (CONV_WIDTH - 1):], v


def t5_bucket(dist):
    max_exact = N_BUCKETS // 2
    d = jnp.maximum(dist, max_exact).astype(jnp.float32)
    large = max_exact + (jnp.log(d / max_exact) / math.log(MAX_DISTANCE / max_exact)
                         * (N_BUCKETS - max_exact)).astype(jnp.int32)
    return jnp.where(dist < max_exact, dist, jnp.minimum(large, N_BUCKETS - 1))


def dist_bias(dist, rel_bias):
    bucket = t5_bucket(jnp.clip(dist, 0, WINDOW - 1))
    b = rel_bias.astype(jnp.float32)[bucket]
    b = b.reshape(*dist.shape, N_KV_HEADS, GQA_GROUP)
    return jnp.transpose(b, (2, 3, 0, 1))


def qkv_split(h, w_qkv):
    proj = h @ w_qkv
    q, k, v = jnp.split(proj, [N_HEADS * HEAD_DIM, (N_HEADS + N_KV_HEADS) * HEAD_DIM], axis=-1)
    lead = h.shape[:-1]
    return (q.reshape(*lead, N_KV_HEADS, GQA_GROUP, HEAD_DIM),
            k.reshape(*lead, N_KV_HEADS, HEAD_DIM),
            v.reshape(*lead, N_KV_HEADS, HEAD_DIM))


def sink_attend(q, k, v, bias, valid, sink):
    s = jnp.einsum('...qkgd,...skd->...kgqs', q, k,
                   preferred_element_type=jnp.float32) * (HEAD_DIM ** -0.5) + bias
    s = jnp.where(valid, s, NEG_INF)
    sk = sink.astype(jnp.float32).reshape(N_KV_HEADS, GQA_GROUP, 1, 1)
    m = jnp.maximum(jnp.max(s, axis=-1, keepdims=True), sk)
    p = jnp.exp(s - m)
    w = p / (jnp.sum(p, axis=-1, keepdims=True) + jnp.exp(sk - m))
    return jnp.einsum('...kgqs,...skd->...qkgd', w.astype(v.dtype), v)


def odd_mixer_prompt(h, w_qkv, w_o, sink, rel_bias):
    n, s, _ = h.shape
    nb = s // ATTN_BLOCK
    q, k, v = qkv_split(h, w_qkv)
    qb = q.reshape(n, nb, ATTN_BLOCK, N_KV_HEADS, GQA_GROUP, HEAD_DIM)

    def band(t):
        tb = t.reshape(n, nb, ATTN_BLOCK, N_KV_HEADS, HEAD_DIM)
        prev = jnp.concatenate([jnp.zeros_like(tb[:, :1]), tb[:, :-1]], axis=1)
        return jnp.concatenate([prev, tb], axis=2)

    kb, vb = band(k), band(v)
    dist = jnp.arange(ATTN_BLOCK)[:, None] + ATTN_BLOCK - jnp.arange(2 * ATTN_BLOCK)[None, :]
    key_pos = (jnp.arange(nb)[:, None] - 1) * ATTN_BLOCK + jnp.arange(2 * ATTN_BLOCK)[None, :]
    valid = (dist >= 0) & (dist < WINDOW) & (key_pos[:, None, :] >= 0)
    o = sink_attend(qb, kb, vb, dist_bias(dist, rel_bias), valid[:, None, None], sink)
    y = o.reshape(n, s, N_HEADS * HEAD_DIM) @ w_o
    return y, k[:, s - WINDOW:], v[:, s - WINDOW:]


def odd_mixer_sample(h, cache_k, cache_v, w_qkv, w_o, sink, rel_bias):
    n, t, _ = h.shape
    r = cache_k.shape[1]
    q, k, v = qkv_split(h, w_qkv)
    k_all = jnp.concatenate([cache_k.astype(k.dtype), k], axis=1)
    v_all = jnp.concatenate([cache_v.astype(v.dtype), v], axis=1)
    dist = jnp.arange(t)[:, None] + r - jnp.arange(r + t)[None, :]
    valid = (dist >= 0) & (dist < WINDOW)
    o = sink_attend(q, k_all, v_all, dist_bias(dist, rel_bias), valid, sink)
    y = o.reshape(n, t, N_HEADS * HEAD_DIM) @ w_o
    return y, k_all[:, k_all.shape[1] - WINDOW:], v_all[:, v_all.shape[1] - WINDOW:]


def setup_inputs(seed: int = 0) -> dict:
    key = jax.random.key(seed)
    ks = jax.random.split(key, 22)
    f32 = jnp.float32

    def nrm(k, shape, scale):
        return jax.random.normal(k, shape, f32) * scale

    win_rows = min(WINDOW, PAST_LEN)
    return {
        "x_prompt": nrm(ks[0], (BATCH, SEQ, D_MODEL), 1.0),
        "x_sample": nrm(ks[1], (DEC_BATCH, DEC_SEQ, D_MODEL), 1.0),
        "state_conv": nrm(ks[2], (N_EVEN, DEC_BATCH, CONV_WIDTH - 1, B_WIDTH), 0.5),
        "cache_win_k": nrm(ks[3], (N_ODD, DEC_BATCH, win_rows, N_KV_HEADS, HEAD_DIM), 1.0),
        "cache_win_v": nrm(ks[4], (N_ODD, DEC_BATCH, win_rows, N_KV_HEADS, HEAD_DIM), 1.0),
        "norm_mix_pre": 1.0 + nrm(ks[5], (DEPTH, D_MODEL), 0.05),
        "norm_mix_post": 1.0 + nrm(ks[6], (DEPTH, D_MODEL), 0.05),
        "norm_ffn_pre": 1.0 + nrm(ks[7], (DEPTH, D_MODEL), 0.05),
        "norm_ffn_post": 1.0 + nrm(ks[8], (DEPTH, D_MODEL), 0.05),
        "w_in_even": nrm(ks[9], (N_EVEN, D_MODEL, EVEN_PROJ), D_MODEL ** -0.5),
        "w_out_even": nrm(ks[10], (N_EVEN, MIX_WIDTH, D_MODEL), MIX_WIDTH ** -0.5),
        "sgu_w": nrm(ks[11], (N_EVEN, A_HEADS, CHUNK, CHUNK), CHUNK ** -0.5),
        "sgu_b": 1.0 + nrm(ks[12], (N_EVEN, A_HEADS, CHUNK), 0.1),
        "conv_w": nrm(ks[13], (N_EVEN, CONV_WIDTH, B_WIDTH), CONV_WIDTH ** -0.5),
        "w_qkv_odd": nrm(ks[14], (N_ODD, D_MODEL, QKV_PROJ), D_MODEL ** -0.5),
        "w_o_odd": nrm(ks[15], (N_ODD, N_HEADS * HEAD_DIM, D_MODEL), (N_HEADS * HEAD_DIM) ** -0.5),
        "attn_sinks": nrm(ks[16], (N_ODD, N_HEADS), 1.0),
        "rel_bias": nrm(ks[17], (N_BUCKETS, N_HEADS), 0.5),
        "ffn_w_gate": nrm(ks[18], (DEPTH, D_MODEL, D_FF), D_MODEL ** -0.5),
        "ffn_w_up": nrm(ks[19], (DEPTH, D_MODEL, D_FF), D_MODEL ** -0.5),
        "ffn_w_down": nrm(ks[20], (DEPTH, D_FF, D_MODEL), D_FF ** -0.5),
    }


def reference(x_prompt, x_sample, state_conv, cache_win_k, cache_win_v,
              norm_mix_pre, norm_mix_post, norm_ffn_pre, norm_ffn_post,
              w_in_even, w_out_even, sgu_w, sgu_b, conv_w,
              w_qkv_odd, w_o_odd, attn_sinks, rel_bias,
              ffn_w_gate, ffn_w_up, ffn_w_down):
    xp, xs = x_prompt, x_sample
    conv_p, conv_s, chunk_v_s = [], [], []
    kp, vp, ks, vs = [], [], [], []
    for layer in range(DEPTH):
        i = layer // 2
        hp = rms_norm(xp, norm_mix_pre[layer])
        hs = rms_norm(xs, norm_mix_pre[layer])
        if layer % 2 == 0:
            yp, cp = even_mixer_prompt(hp, w_in_even[i], w_out_even[i], sgu_w[i], sgu_b[i], conv_w[i])
            ys, cs, v_new = even_mixer_sample(hs, state_conv[i], w_in_even[i], w_out_even[i],
                                              sgu_w[i], sgu_b[i], conv_w[i])
            conv_p.append(cp)
            conv_s.append(cs)
            chunk_v_s.append(v_new)
        else:
            yp, k_p, v_p = odd_mixer_prompt(hp, w_qkv_odd[i], w_o_odd[i], attn_sinks[i], rel_bias)
            ys, k_s, v_s = odd_mixer_sample(hs, cache_win_k[i], cache_win_v[i], w_qkv_odd[i],
                                            w_o_odd[i], attn_sinks[i], rel_bias)
            kp.append(k_p)
            vp.append(v_p)
            ks.append(k_s)
            vs.append(v_s)
        xp = xp + rms_norm(yp, norm_mix_post[layer])
        xs = xs + rms_norm(ys, norm_mix_post[layer])
        xp = xp + rms_norm(swiglu(rms_norm(xp, norm_ffn_pre[layer]), ffn_w_gate[layer],
                                  ffn_w_up[layer], ffn_w_down[layer]), norm_ffn_post[layer])
        xs = xs + rms_norm(swiglu(rms_norm(xs, norm_ffn_pre[layer]), ffn_w_gate[layer],
                                  ffn_w_up[layer], ffn_w_down[layer]), norm_ffn_post[layer])
    return (xp, xs, jnp.stack(conv_p), jnp.stack(conv_s), jnp.stack(kp), jnp.stack(vp),
            jnp.stack(ks), jnp.stack(vs), jnp.stack(chunk_v_s))
```

```python
import functools
import math

import jax
import jax.numpy as jnp
from jax import lax
from jax.experimental import pallas as pl
from jax.experimental.pallas import tpu as pltpu

F32 = jnp.float32
BF16 = jnp.bfloat16

EPS = 1e-6
NEG_INF = -1e30
CHUNK = 128
A_HEADS = 8
N_KV_HEADS = 8
GQA_GROUP = 8
HEAD_DIM = 64
N_BUCKETS = 32
MAX_DISTANCE = 128
SEQ_CHUNKS = 16

VMEM_LIMIT_BYTES = 56 * 1024 * 1024

TM = 640
TR = 320
TN = 512
TN_EVEN = 256
TF = 256
PAD_NEW_KEYS = 16


def _params(*sem):
    return pltpu.CompilerParams(dimension_semantics=sem, vmem_limit_bytes=VMEM_LIMIT_BYTES)


def _rms(x, g):
    return x * lax.rsqrt(jnp.mean(x * x, axis=-1, keepdims=True) + EPS) * g


def _norm_kernel(x_ref, g_ref, xn_ref):
    xn_ref[...] = _rms(x_ref[...], g_ref[...]).astype(BF16)


def _norm(x, g):
    m, d = x.shape
    return pl.pallas_call(
        _norm_kernel,
        grid=(m // TR,),
        in_specs=[pl.BlockSpec((TR, d), lambda i: (i, 0)),
                  pl.BlockSpec((1, d), lambda i: (0, 0))],
        out_specs=pl.BlockSpec((TR, d), lambda i: (i, 0)),
        out_shape=jax.ShapeDtypeStruct((m, d), BF16),
        compiler_params=_params("parallel"),
        name="norm",
    )(x, g.reshape(1, d))


def _resnorm_kernel(x_ref, y_ref, gp_ref, gn_ref, xo_ref, xn_ref):
    xo = x_ref[...] + _rms(y_ref[...], gp_ref[...])
    xo_ref[...] = xo
    xn_ref[...] = _rms(xo, gn_ref[...]).astype(BF16)


def _res_kernel(x_ref, y_ref, gp_ref, xo_ref):
    xo_ref[...] = x_ref[...] + _rms(y_ref[...], gp_ref[...])


def _resnorm(x, y, g_post, g_next):
    m, d = x.shape
    row = pl.BlockSpec((TR, d), lambda i: (i, 0))
    vec = pl.BlockSpec((1, d), lambda i: (0, 0))
    if g_next is None:
        return pl.pallas_call(
            _res_kernel, grid=(m // TR,), in_specs=[row, row, vec], out_specs=row,
            out_shape=jax.ShapeDtypeStruct((m, d), F32),
            compiler_params=_params("parallel"), name="res",
        )(x, y, g_post.reshape(1, d)), None
    return pl.pallas_call(
        _resnorm_kernel, grid=(m // TR,), in_specs=[row, row, vec, vec],
        out_specs=[row, row],
        out_shape=[jax.ShapeDtypeStruct((m, d), F32), jax.ShapeDtypeStruct((m, d), BF16)],
        compiler_params=_params("parallel"), name="resnorm",
    )(x, y, g_post.reshape(1, d), g_next.reshape(1, d))


def _proj_kernel(a_ref, w_ref, o_ref):
    o_ref[...] = jnp.dot(a_ref[...], w_ref[...], preferred_element_type=F32)


def _proj(a, w, name):
    m, k = a.shape
    n = w.shape[1]
    return pl.pallas_call(
        _proj_kernel,
        grid=(m // TM, n // TN),
        in_specs=[pl.BlockSpec((TM, k), lambda i, j: (i, 0)),
                  pl.BlockSpec((k, TN), lambda i, j: (0, j))],
        out_specs=pl.BlockSpec((TM, TN), lambda i, j: (i, j)),
        out_shape=jax.ShapeDtypeStruct((m, n), F32),
        compiler_params=_params("parallel", "arbitrary"),
        name=name,
    )(a, w)


def _even_proj_kernel(xn_ref, wu_ref, wv_ref, wx_ref, wc_ref, wb_ref,
                      u_ref, v_ref, z_ref, gb_ref):
    x = xn_ref[...]

    def dot(w_ref):
        return jnp.dot(x, w_ref[...], preferred_element_type=F32)

    u_ref[...] = jax.nn.gelu(dot(wu_ref), approximate=True)
    v_ref[...] = jax.nn.gelu(dot(wv_ref), approximate=True)
    z_ref[...] = dot(wc_ref) * dot(wx_ref)
    gb_ref[...] = dot(wb_ref)


def _even_proj(xn, w_in):
    m, d = xn.shape
    width = w_in.shape[1] // 5
    nt = width // TN_EVEN

    def wspec(group):
        return pl.BlockSpec((d, TN_EVEN), lambda i, j: (0, group * nt + j))

    out = pl.BlockSpec((TM, TN_EVEN), lambda i, j: (i, j))
    return pl.pallas_call(
        _even_proj_kernel,
        grid=(m // TM, nt),
        in_specs=[pl.BlockSpec((TM, d), lambda i, j: (i, 0)),
                  wspec(0), wspec(1), wspec(2), wspec(3), wspec(4)],
        out_specs=[out] * 4,
        out_shape=[jax.ShapeDtypeStruct((m, width), F32)] * 4,
        compiler_params=_params("parallel", "arbitrary"),
        name="even_proj",
    )(xn, w_in, w_in, w_in, w_in, w_in)


def _even_mix_kernel(u_ref, v_ref, z_ref, gb_ref, zh_ref, w_ref, b_ref, cw_ref, s_ref, ab_ref):
    c = pl.program_id(0)
    n_prompt = pl.num_programs(0) - 1
    hd = u_ref.shape[1] // A_HEADS
    half = u_ref.shape[1]

    v = v_ref[...].astype(BF16)
    for h in range(A_HEADS):
        cols = slice(h * hd, (h + 1) * hd)
        mixed = jnp.dot(w_ref[0, h], v[:, cols], preferred_element_type=F32)
        mixed = mixed + b_ref[0][:, h:h + 1]
        ab_ref[:, cols] = (u_ref[:, cols] * mixed).astype(BF16)

    z = z_ref[...]
    row = lax.broadcasted_iota(jnp.int32, (CHUNK, 1), 0)
    r1 = pltpu.roll(z, 1, 0)
    r2 = pltpu.roll(z, 2, 0)
    cw = cw_ref[...]

    def emit(zp1, zp2):
        conv = cw[0:1] * zp2 + cw[1:2] * zp1 + cw[2:3] * z
        ab_ref[:, half:] = (gb_ref[...] * conv).astype(BF16)

    @pl.when(c < n_prompt)
    def _():
        keep = c % SEQ_CHUNKS != 0
        h1 = jnp.where(keep, zh_ref[7:8, :], 0.0)
        h2 = jnp.where(keep, zh_ref[6:7, :], 0.0)
        emit(jnp.where(row >= 1, r1, h1),
             jnp.where(row >= 2, r2, jnp.where(row == 1, h1, h2)))

    @pl.when(c == n_prompt)
    def _():
        t = row % 4
        emit(jnp.where(t >= 1, r1, s_ref[0]), jnp.where(t >= 2, r2, s_ref[1]))


def _even_mix(u, v, z, gb, wmix, bmix, conv_w, state_rows):
    m, half = u.shape
    nchunks = m // CHUNK
    blk = pl.BlockSpec((CHUNK, half), lambda c: (c, 0))
    sel = lambda c: (c // (nchunks - 1), 0, 0, 0)
    return pl.pallas_call(
        _even_mix_kernel,
        grid=(nchunks,),
        in_specs=[blk, blk, blk, blk,
                  pl.BlockSpec((8, half), lambda c: (jnp.maximum(c * (CHUNK // 8) - 1, 0), 0)),
                  pl.BlockSpec((1, A_HEADS, CHUNK, CHUNK), sel),
                  pl.BlockSpec((1, CHUNK, A_HEADS), lambda c: (c // (nchunks - 1), 0, 0)),
                  pl.BlockSpec((3, half), lambda c: (0, 0)),
                  pl.BlockSpec((2, CHUNK, half), lambda c: (0, 0, 0))],
        out_specs=pl.BlockSpec((CHUNK, 2 * half), lambda c: (c, 0)),
        out_shape=jax.ShapeDtypeStruct((m, 2 * half), BF16),
        compiler_params=_params("arbitrary"),
        name="even_mix",
    )(u, v, z, gb, z, wmix, bmix, conv_w, state_rows)


def _softmax_parts(s, sk):
    m = jnp.maximum(jnp.max(s, axis=-1, keepdims=True), sk)
    p = jnp.exp(s - m)
    return p, jnp.sum(p, axis=-1, keepdims=True) + jnp.exp(sk - m), m


def _attn_prompt_kernel(sink_ref, q_ref, kc_ref, vc_ref, kp_ref, vp_ref, bias_ref, o_ref):
    i = pl.program_id(0)
    kk = jnp.concatenate([kp_ref[...], kc_ref[...]], axis=0).astype(BF16)
    vv = jnp.concatenate([vp_ref[...], vc_ref[...]], axis=0).astype(BF16)
    row = lax.broadcasted_iota(jnp.int32, (CHUNK, 2 * CHUNK), 0)
    col = lax.broadcasted_iota(jnp.int32, (CHUNK, 2 * CHUNK), 1)
    dist = row + CHUNK - col
    has_prev = i % SEQ_CHUNKS != 0
    valid = (dist >= 0) & (dist < CHUNK) & (has_prev | (col >= CHUNK))
    nt = (((1,), (1,)), ((), ()))
    for h in range(N_KV_HEADS * GQA_GROUP):
        g = h // GQA_GROUP
        hc = slice(h * HEAD_DIM, (h + 1) * HEAD_DIM)
        gc = slice(g * HEAD_DIM, (g + 1) * HEAD_DIM)
        q = (q_ref[:, hc] * (HEAD_DIM ** -0.5)).astype(BF16)
        s = lax.dot_general(q, kk[:, gc], nt, preferred_element_type=F32) + bias_ref[h]
        s = jnp.where(valid, s, NEG_INF)
        p, den, _ = _softmax_parts(s, sink_ref[h])
        o = jnp.dot(p.astype(BF16), vv[:, gc], preferred_element_type=F32) / den
        o_ref[:, hc] = o.astype(BF16)


def _attn_prompt(qkv, sinks, bias, n_rows):
    dq = N_KV_HEADS * GQA_GROUP * HEAD_DIM
    dkv = N_KV_HEADS * HEAD_DIM
    kcol, vcol = dq // dkv, dq // dkv + 1
    prev = lambda i: jnp.maximum(i - 1, 0)
    return pl.pallas_call(
        _attn_prompt_kernel,
        grid=(n_rows // CHUNK,),
        in_specs=[pl.BlockSpec(memory_space=pltpu.SMEM),
                  pl.BlockSpec((CHUNK, dq), lambda i: (i, 0)),
                  pl.BlockSpec((CHUNK, dkv), lambda i: (i, kcol)),
                  pl.BlockSpec((CHUNK, dkv), lambda i: (i, vcol)),
                  pl.BlockSpec((CHUNK, dkv), lambda i: (prev(i), kcol)),
                  pl.BlockSpec((CHUNK, dkv), lambda i: (prev(i), vcol)),
                  pl.BlockSpec(bias.shape, lambda i: (0, 0, 0))],
        out_specs=pl.BlockSpec((CHUNK, dq), lambda i: (i, 0)),
        out_shape=jax.ShapeDtypeStruct((n_rows, dq), BF16),
        compiler_params=_params("arbitrary"),
        name="attn_prompt",
    )(sinks, qkv, qkv, qkv, qkv, qkv, bias)


def _attn_sample_kernel(q_ref, kn_ref, vn_ref, kc_ref, vc_ref, bc_ref, bn_ref, sink_ref, o_ref):
    kc = kc_ref[0].astype(BF16)
    vc = vc_ref[0].astype(BF16)
    kn = kn_ref[0].astype(BF16)
    vn = vn_ref[0].astype(BF16)
    rows = q_ref.shape[2]
    dec = rows // GQA_GROUP
    t_c = lax.broadcasted_iota(jnp.int32, (rows, CHUNK), 0) % dec
    j_c = lax.broadcasted_iota(jnp.int32, (rows, CHUNK), 1)
    valid_c = j_c > t_c
    t_n = lax.broadcasted_iota(jnp.int32, (rows, PAD_NEW_KEYS), 0) % dec
    j_n = lax.broadcasted_iota(jnp.int32, (rows, PAD_NEW_KEYS), 1)
    valid_n = j_n <= t_n
    nt = (((1,), (1,)), ((), ()))
    for g in range(N_KV_HEADS):
        gc = slice(g * HEAD_DIM, (g + 1) * HEAD_DIM)
        q = (q_ref[0, g] * (HEAD_DIM ** -0.5)).astype(BF16)
        sc = lax.dot_general(q, kc[:, gc], nt, preferred_element_type=F32) + bc_ref[g]
        sn = lax.dot_general(q, kn[:, gc], nt, preferred_element_type=F32) + bn_ref[g]
        sc = jnp.where(valid_c, sc, NEG_INF)
        sn = jnp.where(valid_n, sn, NEG_INF)
        sk = sink_ref[g]
        m = jnp.maximum(jnp.maximum(jnp.max(sc, axis=-1, keepdims=True),
                                    jnp.max(sn, axis=-1, keepdims=True)), sk)
        pc = jnp.exp(sc - m)
        pn = jnp.exp(sn - m)
        den = (jnp.sum(pc, axis=-1, keepdims=True) + jnp.sum(pn, axis=-1, keepdims=True)
               + jnp.exp(sk - m))
        o = (jnp.dot(pc.astype(BF16), vc[:, gc], preferred_element_type=F32)
             + jnp.dot(pn.astype(BF16), vn[:, gc], preferred_element_type=F32))
        o_ref[0, g] = o / den


def _attn_sample(q, kn, vn, kc, vc, bias_c, bias_n, sink_rows):
    nb, _, rows, _ = q.shape
    dkv = N_KV_HEADS * HEAD_DIM
    full = lambda a: pl.BlockSpec(a.shape, lambda b: (0,) * a.ndim)
    return pl.pallas_call(
        _attn_sample_kernel,
        grid=(nb,),
        in_specs=[pl.BlockSpec((1, N_KV_HEADS, rows, HEAD_DIM), lambda b: (b, 0, 0, 0)),
                  pl.BlockSpec((1, PAD_NEW_KEYS, dkv), lambda b: (b, 0, 0)),
                  pl.BlockSpec((1, PAD_NEW_KEYS, dkv), lambda b: (b, 0, 0)),
                  pl.BlockSpec((1, CHUNK, dkv), lambda b: (b, 0, 0)),
                  pl.BlockSpec((1, CHUNK, dkv), lambda b: (b, 0, 0)),
                  full(bias_c), full(bias_n), full(sink_rows)],
        out_specs=pl.BlockSpec((1, N_KV_HEADS, rows, HEAD_DIM), lambda b: (b, 0, 0, 0)),
        out_shape=jax.ShapeDtypeStruct(q.shape, F32),
        compiler_params=_params("parallel"),
        name="attn_sample",
    )(q, kn, vn, kc, vc, bias_c, bias_n, sink_rows)


def _ffn_kernel(xn_ref, wg_ref, wu_ref, wd_ref, y_ref):
    f = pl.program_id(1)
    x = xn_ref[...]
    gate = jnp.dot(x, wg_ref[...], preferred_element_type=F32)
    up = jnp.dot(x, wu_ref[...], preferred_element_type=F32)
    h = (jax.nn.silu(gate) * up).astype(BF16)

    @pl.when(f == 0)
    def _():
        y_ref[...] = jnp.dot(h, wd_ref[...], preferred_element_type=F32)

    @pl.when(f > 0)
    def _():
        y_ref[...] += jnp.dot(h, wd_ref[...], preferred_element_type=F32)


def _ffn(xn, wg, wu, wd):
    m, d = xn.shape
    dff = wg.shape[1]
    return pl.pallas_call(
        _ffn_kernel,
        grid=(m // TM, dff // TF),
        in_specs=[pl.BlockSpec((TM, d), lambda i, f: (i, 0)),
                  pl.BlockSpec((d, TF), lambda i, f: (0, f)),
                  pl.BlockSpec((d, TF), lambda i, f: (0, f)),
                  pl.BlockSpec((TF, d), lambda i, f: (f, 0))],
        out_specs=pl.BlockSpec((TM, d), lambda i, f: (i, 0)),
        out_shape=jax.ShapeDtypeStruct((m, d), F32),
        compiler_params=_params("parallel", "arbitrary"),
        name="ffn",
    )(xn, wg, wu, wd)


def _t5_bucket(dist):
    max_exact = N_BUCKETS // 2
    d = jnp.maximum(dist, max_exact).astype(F32)
    large = max_exact + (jnp.log(d / max_exact) / math.log(MAX_DISTANCE / max_exact)
                         * (N_BUCKETS - max_exact)).astype(jnp.int32)
    return jnp.where(dist < max_exact, dist, jnp.minimum(large, N_BUCKETS - 1))


def _bias_tables(rel_bias, dec_seq):
    n_heads = rel_bias.shape[1]
    by_dist = rel_bias.astype(F32)[_t5_bucket(jnp.arange(CHUNK))].T
    q = jnp.arange(CHUNK)[:, None]
    dist_p = jnp.clip(q + CHUNK - jnp.arange(2 * CHUNK)[None, :], 0, CHUNK - 1)
    bias_p = by_dist[:, dist_p]
    t = jnp.arange(dec_seq)[:, None]
    dist_c = jnp.clip(t + CHUNK - jnp.arange(CHUNK)[None, :], 0, CHUNK - 1)
    dist_n = jnp.clip(t - jnp.arange(PAD_NEW_KEYS)[None, :], 0, CHUNK - 1)
    rows = GQA_GROUP * dec_seq
    bias_c = by_dist[:, dist_c].reshape(N_KV_HEADS, rows, CHUNK)
    bias_n = by_dist[:, dist_n].reshape(N_KV_HEADS, rows, PAD_NEW_KEYS)
    assert n_heads == N_KV_HEADS * GQA_GROUP
    return bias_p, bias_c, bias_n


def _mix_tables(w_s, b_s, dec_batch, dec_seq):
    assert dec_batch * dec_seq == CHUNK
    w_p = jnp.tril(w_s)
    small = jnp.tril(w_s[:, :dec_seq, :dec_seq])
    eye = jnp.eye(dec_batch, dtype=w_s.dtype)
    w_d = jnp.einsum("bc,hij->hbicj", eye, small).reshape(w_s.shape)
    b_p = b_s.T
    b_d = jnp.tile(b_s[:, :dec_seq].T, (dec_batch, 1))
    return jnp.stack([w_p, w_d]).astype(BF16), jnp.stack([b_p, b_d])


def kernel(x_prompt, x_sample, state_conv, cache_win_k, cache_win_v, norm_mix_pre, norm_mix_post, norm_ffn_pre, norm_ffn_post, w_in_even, w_out_even, sgu_w, sgu_b, conv_w, w_qkv_odd, w_o_odd, attn_sinks, rel_bias, ffn_w_gate, ffn_w_up, ffn_w_down):
    batch, seq, d = x_prompt.shape
    dec_batch, dec_seq, _ = x_sample.shape
    depth = norm_mix_pre.shape[0]
    n_p = batch * seq
    n_s = dec_batch * dec_seq
    dq = N_KV_HEADS * GQA_GROUP * HEAD_DIM
    dkv = N_KV_HEADS * HEAD_DIM
    assert seq == SEQ_CHUNKS * CHUNK and n_s == CHUNK and dec_seq <= PAD_NEW_KEYS

    x = jnp.concatenate([x_prompt.reshape(n_p, d), x_sample.reshape(n_s, d)], axis=0)
    xn = _norm(x, norm_mix_pre[0])
    bias_p, bias_c, bias_n = _bias_tables(rel_bias, dec_seq)

    conv_p, conv_s, chunk_v_s = [], [], []
    win_kp, win_vp, win_ks, win_vs = [], [], [], []
    for layer in range(depth):
        i = layer // 2
        if layer % 2 == 0:
            u, v, z, gb = _even_proj(xn, w_in_even[i].astype(BF16))
            wmix, bmix = _mix_tables(sgu_w[i], sgu_b[i], dec_batch, dec_seq)
            st = state_conv[i]
            zero = jnp.zeros_like(st[:, :1])
            s1 = jnp.concatenate([st[:, 1:2], zero, zero, zero], axis=1)
            s2 = jnp.concatenate([st[:, 0:1], st[:, 1:2], zero, zero], axis=1)
            state_rows = jnp.stack([s1.reshape(n_s, -1), s2.reshape(n_s, -1)])
            a = _even_mix(u, v, z, gb, wmix, bmix, conv_w[i], state_rows)
            y = _proj(a, w_out_even[i].astype(BF16), "even_out")
            zc = z.shape[1]
            conv_p.append(z[:n_p].reshape(batch, seq, zc)[:, seq - 2:])
            conv_s.append(z[n_p:].reshape(dec_batch, dec_seq, zc)[:, dec_seq - 2:])
            chunk_v_s.append(v[n_p:].reshape(dec_batch, dec_seq, A_HEADS, zc // A_HEADS))
        else:
            qkv = _proj(xn, w_qkv_odd[i].astype(BF16), "qkv")
            o_p = _attn_prompt(qkv, attn_sinks[i], bias_p, n_p)
            qs = qkv[n_p:, :dq].reshape(dec_batch, dec_seq, N_KV_HEADS, GQA_GROUP, HEAD_DIM)
            qs = qs.transpose(0, 2, 3, 1, 4).reshape(dec_batch, N_KV_HEADS, GQA_GROUP * dec_seq, HEAD_DIM)
            k_new = qkv[n_p:, dq:dq + dkv].reshape(dec_batch, dec_seq, dkv)
            v_new = qkv[n_p:, dq + dkv:].reshape(dec_batch, dec_seq, dkv)
            pad = ((0, 0), (0, PAD_NEW_KEYS - dec_seq), (0, 0))
            sink_rows = jnp.repeat(attn_sinks[i].astype(F32), dec_seq).reshape(N_KV_HEADS, GQA_GROUP * dec_seq, 1)
            o_s = _attn_sample(qs, jnp.pad(k_new, pad), jnp.pad(v_new, pad),
                               cache_win_k[i].reshape(dec_batch, -1, dkv),
                               cache_win_v[i].reshape(dec_batch, -1, dkv),
                               bias_c, bias_n, sink_rows)
            o_s = o_s.reshape(dec_batch, N_KV_HEADS, GQA_GROUP, dec_seq, HEAD_DIM)
            o_s = o_s.transpose(0, 3, 1, 2, 4).reshape(n_s, dq).astype(BF16)
            y = _proj(jnp.concatenate([o_p, o_s], axis=0), w_o_odd[i].astype(BF16), "attn_out")
            win = CHUNK
            k_p = qkv[:n_p, dq:dq + dkv].reshape(batch, seq, N_KV_HEADS, HEAD_DIM)
            v_p = qkv[:n_p, dq + dkv:].reshape(batch, seq, N_KV_HEADS, HEAD_DIM)
            win_kp.append(k_p[:, seq - win:])
            win_vp.append(v_p[:, seq - win:])
            k_all = jnp.concatenate([cache_win_k[i], k_new.reshape(dec_batch, dec_seq, N_KV_HEADS, HEAD_DIM)], axis=1)
            v_all = jnp.concatenate([cache_win_v[i], v_new.reshape(dec_batch, dec_seq, N_KV_HEADS, HEAD_DIM)], axis=1)
            win_ks.append(k_all[:, k_all.shape[1] - win:])
            win_vs.append(v_all[:, v_all.shape[1] - win:])
        x, xn = _resnorm(x, y, norm_mix_post[layer], norm_ffn_pre[layer])
        y = _ffn(xn, ffn_w_gate[layer].astype(BF16), ffn_w_up[layer].astype(BF16),
                 ffn_w_down[layer].astype(BF16))
        g_next = norm_mix_pre[layer + 1] if layer + 1 < depth else None
        x, xn = _resnorm(x, y, norm_ffn_post[layer], g_next)

    return (x[:n_p].reshape(batch, seq, d), x[n_p:].reshape(dec_batch, dec_seq, d),
            jnp.stack(conv_p), jnp.stack(conv_s), jnp.stack(win_kp), jnp.stack(win_vp),
            jnp.stack(win_ks), jnp.stack(win_vs), jnp.stack(chunk_v_s))
```

```python
import functools
import math

import jax
import jax.numpy as jnp
from jax import lax
from jax.experimental import pallas as pl
from jax.experimental.pallas import tpu as pltpu

F32 = jnp.float32
BF16 = jnp.bfloat16

EPS = 1e-6
NEG_INF = -1e30
CHUNK = 128
A_HEADS = 8
N_KV_HEADS = 8
GQA_GROUP = 8
HEAD_DIM = 64
N_BUCKETS = 32
MAX_DISTANCE = 128
SEQ_CHUNKS = 16

VMEM_LIMIT_BYTES = 56 * 1024 * 1024

TM = 640
TR = 320
TN = 512
TN_EVEN = 256
TF = 256
TC = 256
PAD_NEW_KEYS = 16


def _params(*sem):
    return pltpu.CompilerParams(dimension_semantics=sem, vmem_limit_bytes=VMEM_LIMIT_BYTES)


def _rms(x, g):
    return x * lax.rsqrt(jnp.mean(x * x, axis=-1, keepdims=True) + EPS) * g


def _norm_kernel(x_ref, g_ref, xn_ref):
    xn_ref[...] = _rms(x_ref[...], g_ref[...]).astype(BF16)


def _norm(x, g):
    m, d = x.shape
    return pl.pallas_call(
        _norm_kernel,
        grid=(m // TR,),
        in_specs=[pl.BlockSpec((TR, d), lambda i: (i, 0)),
                  pl.BlockSpec((1, d), lambda i: (0, 0))],
        out_specs=pl.BlockSpec((TR, d), lambda i: (i, 0)),
        out_shape=jax.ShapeDtypeStruct((m, d), BF16),
        compiler_params=_params("parallel"),
        name="norm",
    )(x, g.reshape(1, d))


def _resnorm_kernel(x_ref, y_ref, gp_ref, gn_ref, xo_ref, xn_ref):
    xo = x_ref[...] + _rms(y_ref[...], gp_ref[...])
    xo_ref[...] = xo
    xn_ref[...] = _rms(xo, gn_ref[...]).astype(BF16)


def _res_kernel(x_ref, y_ref, gp_ref, xo_ref):
    xo_ref[...] = x_ref[...] + _rms(y_ref[...], gp_ref[...])


def _resnorm(x, y, g_post, g_next):
    m, d = x.shape
    row = pl.BlockSpec((TR, d), lambda i: (i, 0))
    vec = pl.BlockSpec((1, d), lambda i: (0, 0))
    if g_next is None:
        return pl.pallas_call(
            _res_kernel, grid=(m // TR,), in_specs=[row, row, vec], out_specs=row,
            out_shape=jax.ShapeDtypeStruct((m, d), F32),
            compiler_params=_params("parallel"), name="res",
        )(x, y, g_post.reshape(1, d)), None
    return pl.pallas_call(
        _resnorm_kernel, grid=(m // TR,), in_specs=[row, row, vec, vec],
        out_specs=[row, row],
        out_shape=[jax.ShapeDtypeStruct((m, d), F32), jax.ShapeDtypeStruct((m, d), BF16)],
        compiler_params=_params("parallel"), name="resnorm",
    )(x, y, g_post.reshape(1, d), g_next.reshape(1, d))


def _cast_kernel(w_ref, o_ref):
    o_ref[...] = w_ref[...].astype(BF16)


def _cast(w, layer):
    _, r, c = w.shape
    return pl.pallas_call(
        _cast_kernel,
        grid=(r // TC,),
        in_specs=[pl.BlockSpec((None, TC, c), lambda i: (layer, i, 0))],
        out_specs=pl.BlockSpec((TC, c), lambda i: (i, 0)),
        out_shape=jax.ShapeDtypeStruct((r, c), BF16),
        compiler_params=_params("parallel"),
        name="cast",
    )(w)


def _proj_kernel(a_ref, w_ref, o_ref):
    o_ref[...] = jnp.dot(a_ref[...], w_ref[...], preferred_element_type=F32)


def _proj(a, w, name):
    m, k = a.shape
    n = w.shape[1]
    return pl.pallas_call(
        _proj_kernel,
        grid=(m // TM, n // TN),
        in_specs=[pl.BlockSpec((TM, k), lambda i, j: (i, 0)),
                  pl.BlockSpec((k, TN), lambda i, j: (0, j))],
        out_specs=pl.BlockSpec((TM, TN), lambda i, j: (i, j)),
        out_shape=jax.ShapeDtypeStruct((m, n), F32),
        compiler_params=_params("parallel", "arbitrary"),
        name=name,
    )(a, w)


def _even_proj_kernel(xn_ref, wu_ref, wv_ref, wx_ref, wc_ref, wb_ref,
                      u_ref, v_ref, z_ref, gb_ref):
    x = xn_ref[...]

    def dot(w_ref):
        return jnp.dot(x, w_ref[...], preferred_element_type=F32)

    u_ref[...] = jax.nn.gelu(dot(wu_ref), approximate=True)
    v_ref[...] = jax.nn.gelu(dot(wv_ref), approximate=True)
    z_ref[...] = dot(wc_ref) * dot(wx_ref)
    gb_ref[...] = dot(wb_ref)


def _even_proj(xn, w_in):
    m, d = xn.shape
    width = w_in.shape[1] // 5
    nt = width // TN_EVEN

    def wspec(group):
        return pl.BlockSpec((d, TN_EVEN), lambda i, j: (0, group * nt + j))

    out = pl.BlockSpec((TM, TN_EVEN), lambda i, j: (i, j))
    return pl.pallas_call(
        _even_proj_kernel,
        grid=(m // TM, nt),
        in_specs=[pl.BlockSpec((TM, d), lambda i, j: (i, 0)),
                  wspec(0), wspec(1), wspec(2), wspec(3), wspec(4)],
        out_specs=[out] * 4,
        out_shape=[jax.ShapeDtypeStruct((m, width), F32)] * 4,
        compiler_params=_params("parallel", "arbitrary"),
        name="even_proj",
    )(xn, w_in, w_in, w_in, w_in, w_in)


def _even_mix_kernel(u_ref, v_ref, z_ref, gb_ref, zh_ref, w_ref, b_ref, cw_ref, s_ref, ab_ref):
    c = pl.program_id(0)
    n_prompt = pl.num_programs(0) - 1
    hd = u_ref.shape[1] // A_HEADS
    half = u_ref.shape[1]

    v = v_ref[...].astype(BF16)
    for h in range(A_HEADS):
        cols = slice(h * hd, (h + 1) * hd)
        mixed = jnp.dot(w_ref[0, h], v[:, cols], preferred_element_type=F32)
        mixed = mixed + b_ref[0][:, h:h + 1]
        ab_ref[:, cols] = (u_ref[:, cols] * mixed).astype(BF16)

    z = z_ref[...]
    row = lax.broadcasted_iota(jnp.int32, (CHUNK, 1), 0)
    r1 = pltpu.roll(z, 1, 0)
    r2 = pltpu.roll(z, 2, 0)
    cw = cw_ref[...]

    def emit(zp1, zp2):
        conv = cw[0:1] * zp2 + cw[1:2] * zp1 + cw[2:3] * z
        ab_ref[:, half:] = (gb_ref[...] * conv).astype(BF16)

    @pl.when(c < n_prompt)
    def _():
        keep = c % SEQ_CHUNKS != 0
        h1 = jnp.where(keep, zh_ref[7:8, :], 0.0)
        h2 = jnp.where(keep, zh_ref[6:7, :], 0.0)
        emit(jnp.where(row >= 1, r1, h1),
             jnp.where(row >= 2, r2, jnp.where(row == 1, h1, h2)))

    @pl.when(c == n_prompt)
    def _():
        t = row % 4
        emit(jnp.where(t >= 1, r1, s_ref[0]), jnp.where(t >= 2, r2, s_ref[1]))


def _even_mix(u, v, z, gb, wmix, bmix, conv_w, state_rows):
    m, half = u.shape
    nchunks = m // CHUNK
    blk = pl.BlockSpec((CHUNK, half), lambda c: (c, 0))
    sel = lambda c: (c // (nchunks - 1), 0, 0, 0)
    return pl.pallas_call(
        _even_mix_kernel,
        grid=(nchunks,),
        in_specs=[blk, blk, blk, blk,
                  pl.BlockSpec((8, half), lambda c: (jnp.maximum(c * (CHUNK // 8) - 1, 0), 0)),
                  pl.BlockSpec((1, A_HEADS, CHUNK, CHUNK), sel),
                  pl.BlockSpec((1, CHUNK, A_HEADS), lambda c: (c // (nchunks - 1), 0, 0)),
                  pl.BlockSpec((3, half), lambda c: (0, 0)),
                  pl.BlockSpec((2, CHUNK, half), lambda c: (0, 0, 0))],
        out_specs=pl.BlockSpec((CHUNK, 2 * half), lambda c: (c, 0)),
        out_shape=jax.ShapeDtypeStruct((m, 2 * half), BF16),
        compiler_params=_params("arbitrary"),
        name="even_mix",
    )(u, v, z, gb, z, wmix, bmix, conv_w, state_rows)


def _attn_prompt_kernel(q_ref, kc_ref, vc_ref, kp_ref, vp_ref, bias_ref, sink_ref, o_ref):
    i = pl.program_id(0)
    pair_w = 2 * HEAD_DIM
    kk = jnp.concatenate([kp_ref[...], kc_ref[...]], axis=0).astype(BF16)
    vt = jnp.concatenate([vp_ref[...], vc_ref[...]], axis=0).T.astype(BF16)
    key = lax.broadcasted_iota(jnp.int32, (2 * CHUNK, 2 * CHUNK), 0)
    qry = lax.broadcasted_iota(jnp.int32, (2 * CHUNK, 2 * CHUNK), 1) % CHUNK
    dist = qry + CHUNK - key
    has_prev = i % SEQ_CHUNKS != 0
    valid = (dist >= 0) & (dist < CHUNK) & (has_prev | (key >= CHUNK))
    for p in range(N_KV_HEADS * GQA_GROUP // 2):
        g = (2 * p) // GQA_GROUP
        gc = slice(g * HEAD_DIM, (g + 1) * HEAD_DIM)
        pc = slice(p * pair_w, (p + 1) * pair_w)
        qt = (q_ref[:, pc].T * (HEAD_DIM ** -0.5)).astype(BF16)
        qt = jnp.concatenate([qt[:HEAD_DIM], qt[HEAD_DIM:]], axis=1)
        s = jnp.dot(kk[:, gc], qt, preferred_element_type=F32) + bias_ref[p]
        s = jnp.where(valid, s, NEG_INF)
        sk = sink_ref[p]
        m = jnp.maximum(jnp.max(s, axis=0, keepdims=True), sk)
        e = jnp.exp(s - m)
        den = jnp.sum(e, axis=0, keepdims=True) + jnp.exp(sk - m)
        ot = jnp.dot(vt[gc, :], e.astype(BF16), preferred_element_type=F32) / den
        o = jnp.concatenate([ot[:, :CHUNK], ot[:, CHUNK:]], axis=0).T
        o_ref[:, pc] = o.astype(BF16)


def _attn_prompt(qkv, bias_t, sink_rows, n_rows):
    dq = N_KV_HEADS * GQA_GROUP * HEAD_DIM
    dkv = N_KV_HEADS * HEAD_DIM
    kcol, vcol = dq // dkv, dq // dkv + 1
    prev = lambda i: jnp.maximum(i - 1, 0)
    return pl.pallas_call(
        _attn_prompt_kernel,
        grid=(n_rows // CHUNK,),
        in_specs=[pl.BlockSpec((CHUNK, dq), lambda i: (i, 0)),
                  pl.BlockSpec((CHUNK, dkv), lambda i: (i, kcol)),
                  pl.BlockSpec((CHUNK, dkv), lambda i: (i, vcol)),
                  pl.BlockSpec((CHUNK, dkv), lambda i: (prev(i), kcol)),
                  pl.BlockSpec((CHUNK, dkv), lambda i: (prev(i), vcol)),
                  pl.BlockSpec(bias_t.shape, lambda i: (0, 0, 0)),
                  pl.BlockSpec(sink_rows.shape, lambda i: (0, 0, 0))],
        out_specs=pl.BlockSpec((CHUNK, dq), lambda i: (i, 0)),
        out_shape=jax.ShapeDtypeStruct((n_rows, dq), BF16),
        compiler_params=_params("arbitrary"),
        name="attn_prompt",
    )(qkv, qkv, qkv, qkv, qkv, bias_t, sink_rows)


def _attn_sample_kernel(q_ref, kn_ref, vn_ref, kc_ref, vc_ref, bc_ref, bn_ref, sink_ref, o_ref):
    kc = kc_ref[0].astype(BF16)
    vc = vc_ref[0].astype(BF16)
    kn = kn_ref[0].astype(BF16)
    vn = vn_ref[0].astype(BF16)
    rows = q_ref.shape[2]
    dec = rows // GQA_GROUP
    t_c = lax.broadcasted_iota(jnp.int32, (rows, CHUNK), 0) % dec
    j_c = lax.broadcasted_iota(jnp.int32, (rows, CHUNK), 1)
    valid_c = j_c > t_c
    t_n = lax.broadcasted_iota(jnp.int32, (rows, PAD_NEW_KEYS), 0) % dec
    j_n = lax.broadcasted_iota(jnp.int32, (rows, PAD_NEW_KEYS), 1)
    valid_n = j_n <= t_n
    nt = (((1,), (1,)), ((), ()))
    for g in range(N_KV_HEADS):
        gc = slice(g * HEAD_DIM, (g + 1) * HEAD_DIM)
        q = (q_ref[0, g] * (HEAD_DIM ** -0.5)).astype(BF16)
        sc = lax.dot_general(q, kc[:, gc], nt, preferred_element_type=F32) + bc_ref[g]
        sn = lax.dot_general(q, kn[:, gc], nt, preferred_element_type=F32) + bn_ref[g]
        sc = jnp.where(valid_c, sc, NEG_INF)
        sn = jnp.where(valid_n, sn, NEG_INF)
        sk = sink_ref[g]
        m = jnp.maximum(jnp.maximum(jnp.max(sc, axis=-1, keepdims=True),
                                    jnp.max(sn, axis=-1, keepdims=True)), sk)
        pc = jnp.exp(sc - m)
        pn = jnp.exp(sn - m)
        den = (jnp.sum(pc, axis=-1, keepdims=True) + jnp.sum(pn, axis=-1, keepdims=True)
               + jnp.exp(sk - m))
        o = (jnp.dot(pc.astype(BF16), vc[:, gc], preferred_element_type=F32)
             + jnp.dot(pn.astype(BF16), vn[:, gc], preferred_element_type=F32))
        o_ref[0, g] = o / den


def _attn_sample(q, kn, vn, kc, vc, bias_c, bias_n, sink_rows):
    nb, _, rows, _ = q.shape
    dkv = N_KV_HEADS * HEAD_DIM
    full = lambda a: pl.BlockSpec(a.shape, lambda b: (0,) * a.ndim)
    return pl.pallas_call(
        _attn_sample_kernel,
        grid=(nb,),
        in_specs=[pl.BlockSpec((1, N_KV_HEADS, rows, HEAD_DIM), lambda b: (b, 0, 0, 0)),
                  pl.BlockSpec((1, PAD_NEW_KEYS, dkv), lambda b: (b, 0, 0)),
                  pl.BlockSpec((1, PAD_NEW_KEYS, dkv), lambda b: (b, 0, 0)),
                  pl.BlockSpec((1, CHUNK, dkv), lambda b: (b, 0, 0)),
                  pl.BlockSpec((1, CHUNK, dkv), lambda b: (b, 0, 0)),
                  full(bias_c), full(bias_n), full(sink_rows)],
        out_specs=pl.BlockSpec((1, N_KV_HEADS, rows, HEAD_DIM), lambda b: (b, 0, 0, 0)),
        out_shape=jax.ShapeDtypeStruct(q.shape, F32),
        compiler_params=_params("parallel"),
        name="attn_sample",
    )(q, kn, vn, kc, vc, bias_c, bias_n, sink_rows)


def _ffn_kernel(xn_ref, wg_ref, wu_ref, wd_ref, y_ref):
    f = pl.program_id(1)
    x = xn_ref[...]
    gate = jnp.dot(x, wg_ref[...], preferred_element_type=F32)
    up = jnp.dot(x, wu_ref[...], preferred_element_type=F32)
    h = (jax.nn.silu(gate) * up).astype(BF16)

    @pl.when(f == 0)
    def _():
        y_ref[...] = jnp.dot(h, wd_ref[...], preferred_element_type=F32)

    @pl.when(f > 0)
    def _():
        y_ref[...] += jnp.dot(h, wd_ref[...], preferred_element_type=F32)


def _ffn(xn, wg, wu, wd):
    m, d = xn.shape
    dff = wg.shape[1]
    return pl.pallas_call(
        _ffn_kernel,
        grid=(m // TM, dff // TF),
        in_specs=[pl.BlockSpec((TM, d), lambda i, f: (i, 0)),
                  pl.BlockSpec((d, TF), lambda i, f: (0, f)),
                  pl.BlockSpec((d, TF), lambda i, f: (0, f)),
                  pl.BlockSpec((TF, d), lambda i, f: (f, 0))],
        out_specs=pl.BlockSpec((TM, d), lambda i, f: (i, 0)),
        out_shape=jax.ShapeDtypeStruct((m, d), F32),
        compiler_params=_params("parallel", "arbitrary"),
        name="ffn",
    )(xn, wg, wu, wd)


def _t5_bucket(dist):
    max_exact = N_BUCKETS // 2
    d = jnp.maximum(dist, max_exact).astype(F32)
    large = max_exact + (jnp.log(d / max_exact) / math.log(MAX_DISTANCE / max_exact)
                         * (N_BUCKETS - max_exact)).astype(jnp.int32)
    return jnp.where(dist < max_exact, dist, jnp.minimum(large, N_BUCKETS - 1))


def _bias_tables(rel_bias, dec_seq):
    n_heads = rel_bias.shape[1]
    assert n_heads == N_KV_HEADS * GQA_GROUP
    by_dist = rel_bias.astype(F32)[_t5_bucket(jnp.arange(CHUNK))].T
    span = 3 * CHUNK
    row = by_dist[:, jnp.clip(jnp.arange(span) - (CHUNK - 1), 0, CHUNK - 1)]
    flat = jnp.broadcast_to(row[:, None, :], (n_heads, 2 * CHUNK, span)).reshape(n_heads, -1)
    start = 2 * CHUNK - 1
    bias_t = flat[:, start:start + 2 * CHUNK * (span - 1)].reshape(n_heads, 2 * CHUNK, span - 1)
    bias_t = bias_t[:, :, :CHUNK]
    bias_t = bias_t.reshape(n_heads // 2, 2, 2 * CHUNK, CHUNK).transpose(0, 2, 1, 3)
    bias_t = bias_t.reshape(n_heads // 2, 2 * CHUNK, 2 * CHUNK)
    t = jnp.arange(dec_seq)[:, None]
    dist_c = jnp.clip(t + CHUNK - jnp.arange(CHUNK)[None, :], 0, CHUNK - 1)
    dist_n = jnp.clip(t - jnp.arange(PAD_NEW_KEYS)[None, :], 0, CHUNK - 1)
    rows = GQA_GROUP * dec_seq
    bias_c = by_dist[:, dist_c].reshape(N_KV_HEADS, rows, CHUNK)
    bias_n = by_dist[:, dist_n].reshape(N_KV_HEADS, rows, PAD_NEW_KEYS)
    return bias_t, bias_c, bias_n


def _mix_tables(w_s, b_s, dec_batch, dec_seq):
    assert dec_batch * dec_seq == CHUNK
    w_p = jnp.tril(w_s)
    small = jnp.tril(w_s[:, :dec_seq, :dec_seq])
    eye = jnp.eye(dec_batch, dtype=w_s.dtype)
    w_d = jnp.einsum("bc,hij->hbicj", eye, small).reshape(w_s.shape)
    b_p = b_s.T
    b_d = jnp.tile(b_s[:, :dec_seq].T, (dec_batch, 1))
    return jnp.stack([w_p, w_d]).astype(BF16), jnp.stack([b_p, b_d])


def kernel(x_prompt, x_sample, state_conv, cache_win_k, cache_win_v, norm_mix_pre, norm_mix_post, norm_ffn_pre, norm_ffn_post, w_in_even, w_out_even, sgu_w, sgu_b, conv_w, w_qkv_odd, w_o_odd, attn_sinks, rel_bias, ffn_w_gate, ffn_w_up, ffn_w_down):
    batch, seq, d = x_prompt.shape
    dec_batch, dec_seq, _ = x_sample.shape
    depth = norm_mix_pre.shape[0]
    n_p = batch * seq
    n_s = dec_batch * dec_seq
    dq = N_KV_HEADS * GQA_GROUP * HEAD_DIM
    dkv = N_KV_HEADS * HEAD_DIM
    assert seq == SEQ_CHUNKS * CHUNK and n_s == CHUNK and dec_seq <= PAD_NEW_KEYS

    x = jnp.concatenate([x_prompt.reshape(n_p, d), x_sample.reshape(n_s, d)], axis=0)
    xn = _norm(x, norm_mix_pre[0])
    bias_t, bias_c, bias_n = _bias_tables(rel_bias, dec_seq)

    conv_p, conv_s, chunk_v_s = [], [], []
    win_kp, win_vp, win_ks, win_vs = [], [], [], []
    for layer in range(depth):
        i = layer // 2
        if layer % 2 == 0:
            u, v, z, gb = _even_proj(xn, _cast(w_in_even, i))
            wmix, bmix = _mix_tables(sgu_w[i], sgu_b[i], dec_batch, dec_seq)
            st = state_conv[i]
            zero = jnp.zeros_like(st[:, :1])
            s1 = jnp.concatenate([st[:, 1:2], zero, zero, zero], axis=1)
            s2 = jnp.concatenate([st[:, 0:1], st[:, 1:2], zero, zero], axis=1)
            state_rows = jnp.stack([s1.reshape(n_s, -1), s2.reshape(n_s, -1)])
            a = _even_mix(u, v, z, gb, wmix, bmix, conv_w[i], state_rows)
            y = _proj(a, _cast(w_out_even, i), "even_out")
            zc = z.shape[1]
            conv_p.append(z[:n_p].reshape(batch, seq, zc)[:, seq - 2:])
            conv_s.append(z[n_p:].reshape(dec_batch, dec_seq, zc)[:, dec_seq - 2:])
            chunk_v_s.append(v[n_p:].reshape(dec_batch, dec_seq, A_HEADS, zc // A_HEADS))
        else:
            qkv = _proj(xn, _cast(w_qkv_odd, i), "qkv")
            sink_pairs = jnp.repeat(attn_sinks[i].astype(F32), CHUNK).reshape(-1, 1, 2 * CHUNK)
            o_p = _attn_prompt(qkv, bias_t, sink_pairs, n_p)
            qs = qkv[n_p:, :dq].reshape(dec_batch, dec_seq, N_KV_HEADS, GQA_GROUP, HEAD_DIM)
            qs = qs.transpose(0, 2, 3, 1, 4).reshape(dec_batch, N_KV_HEADS, GQA_GROUP * dec_seq, HEAD_DIM)
            k_new = qkv[n_p:, dq:dq + dkv].reshape(dec_batch, dec_seq, dkv)
            v_new = qkv[n_p:, dq + dkv:].reshape(dec_batch, dec_seq, dkv)
            pad = ((0, 0), (0, PAD_NEW_KEYS - dec_seq), (0, 0))
            sink_rows = jnp.repeat(attn_sinks[i].astype(F32), dec_seq).reshape(N_KV_HEADS, GQA_GROUP * dec_seq, 1)
            o_s = _attn_sample(qs, jnp.pad(k_new, pad), jnp.pad(v_new, pad),
                               cache_win_k[i].reshape(dec_batch, -1, dkv),
                               cache_win_v[i].reshape(dec_batch, -1, dkv),
                               bias_c, bias_n, sink_rows)
            o_s = o_s.reshape(dec_batch, N_KV_HEADS, GQA_GROUP, dec_seq, HEAD_DIM)
            o_s = o_s.transpose(0, 3, 1, 2, 4).reshape(n_s, dq).astype(BF16)
            y = _proj(jnp.concatenate([o_p, o_s], axis=0), _cast(w_o_odd, i), "attn_out")
            win = CHUNK
            k_p = qkv[:n_p, dq:dq + dkv].reshape(batch, seq, N_KV_HEADS, HEAD_DIM)
            v_p = qkv[:n_p, dq + dkv:].reshape(batch, seq, N_KV_HEADS, HEAD_DIM)
            win_kp.append(k_p[:, seq - win:])
            win_vp.append(v_p[:, seq - win:])
            k_all = jnp.concatenate([cache_win_k[i], k_new.reshape(dec_batch, dec_seq, N_KV_HEADS, HEAD_DIM)], axis=1)
            v_all = jnp.concatenate([cache_win_v[i], v_new.reshape(dec_batch, dec_seq, N_KV_HEADS, HEAD_DIM)], axis=1)
            win_ks.append(k_all[:, k_all.shape[1] - win:])
            win_vs.append(v_all[:, v_all.shape[1] - win:])
        x, xn = _resnorm(x, y, norm_mix_post[layer], norm_ffn_pre[layer])
        y = _ffn(xn, _cast(ffn_w_gate, layer), _cast(ffn_w_up, layer), _cast(ffn_w_down, layer))
        g_next = norm_mix_pre[layer + 1] if layer + 1 < depth else None
        x, xn = _resnorm(x, y, norm_ffn_post[layer], g_next)

    return (x[:n_p].reshape(batch, seq, d), x[n_p:].reshape(dec_batch, dec_seq, d),
            jnp.stack(conv_p), jnp.stack(conv_s), jnp.stack(win_kp), jnp.stack(win_vp),
            jnp.stack(win_ks), jnp.stack(win_vs), jnp.stack(chunk_v_s))
```

```python
import functools
import math

import jax
import jax.numpy as jnp
from jax import lax
from jax.experimental import pallas as pl
from jax.experimental.pallas import tpu as pltpu

F32 = jnp.float32
BF16 = jnp.bfloat16

EPS = 1e-6
NEG_INF = -1e30
CHUNK = 128
A_HEADS = 8
N_KV_HEADS = 8
GQA_GROUP = 8
HEAD_DIM = 64
N_BUCKETS = 32
MAX_DISTANCE = 128
SEQ_CHUNKS = 16

VMEM_LIMIT_BYTES = 56 * 1024 * 1024

TM = 640
TR = 320
TN = 512
TN_EVEN = 256
TF = 256
TM_FFN = 1040
TC = 256
PAD_NEW_KEYS = 16


def _params(*sem):
    return pltpu.CompilerParams(dimension_semantics=sem, vmem_limit_bytes=VMEM_LIMIT_BYTES)


def _rms(x, g):
    return x * lax.rsqrt(jnp.mean(x * x, axis=-1, keepdims=True) + EPS) * g


def _two_source_specs(d, n_blocks):
    return [pl.BlockSpec((CHUNK, d), lambda i: (jnp.minimum(i, n_blocks - 2), 0)),
            pl.BlockSpec((CHUNK, d), lambda i: (0, 0))]


def _on_source(body, xp_ref, xs_ref):
    i = pl.program_id(0)
    last = pl.num_programs(0) - 1

    @pl.when(i < last)
    def _():
        body(xp_ref[...])

    @pl.when(i == last)
    def _():
        body(xs_ref[...])


def _norm_first_kernel(xp_ref, xs_ref, g_ref, xn_ref):
    def body(x):
        xn_ref[...] = _rms(x, g_ref[...]).astype(BF16)

    _on_source(body, xp_ref, xs_ref)


def _norm_first(xp, xs, g):
    d = xp.shape[1]
    m = xp.shape[0] + xs.shape[0]
    nb = m // CHUNK
    row = pl.BlockSpec((CHUNK, d), lambda i: (i, 0))
    vec = pl.BlockSpec((1, d), lambda i: (0, 0))
    return pl.pallas_call(
        _norm_first_kernel, grid=(nb,), in_specs=_two_source_specs(d, nb) + [vec], out_specs=row,
        out_shape=jax.ShapeDtypeStruct((m, d), BF16),
        compiler_params=_params("arbitrary"), name="norm_first",
    )(xp, xs, g.reshape(1, d))


def _resnorm_first_kernel(xp_ref, xs_ref, y_ref, gp_ref, gn_ref, xo_ref, xn_ref):
    def body(x):
        xo = x + _rms(y_ref[...], gp_ref[...])
        xo_ref[...] = xo
        xn_ref[...] = _rms(xo, gn_ref[...]).astype(BF16)

    _on_source(body, xp_ref, xs_ref)


def _resnorm_first(xp, xs, y, g_post, g_next):
    m, d = y.shape
    nb = m // CHUNK
    row = pl.BlockSpec((CHUNK, d), lambda i: (i, 0))
    vec = pl.BlockSpec((1, d), lambda i: (0, 0))
    return pl.pallas_call(
        _resnorm_first_kernel, grid=(nb,), in_specs=_two_source_specs(d, nb) + [row, vec, vec],
        out_specs=[row, row],
        out_shape=[jax.ShapeDtypeStruct((m, d), F32), jax.ShapeDtypeStruct((m, d), BF16)],
        compiler_params=_params("arbitrary"), name="resnorm_first",
    )(xp, xs, y, g_post.reshape(1, d), g_next.reshape(1, d))


def _resnorm_kernel(x_ref, y_ref, gp_ref, gn_ref, xo_ref, xn_ref):
    xo = x_ref[...] + _rms(y_ref[...], gp_ref[...])
    xo_ref[...] = xo
    xn_ref[...] = _rms(xo, gn_ref[...]).astype(BF16)


def _resnorm(x, y, g_post, g_next):
    m, d = x.shape
    row = pl.BlockSpec((TR, d), lambda i: (i, 0))
    vec = pl.BlockSpec((1, d), lambda i: (0, 0))
    return pl.pallas_call(
        _resnorm_kernel, grid=(m // TR,), in_specs=[row, row, vec, vec],
        out_specs=[row, row],
        out_shape=[jax.ShapeDtypeStruct((m, d), F32), jax.ShapeDtypeStruct((m, d), BF16)],
        compiler_params=_params("parallel"), name="resnorm",
    )(x, y, g_post.reshape(1, d), g_next.reshape(1, d))


def _res_last_kernel(x_ref, y_ref, gp_ref, op_ref, os_ref):
    i = pl.program_id(0)
    last = pl.num_programs(0) - 1
    xo = x_ref[...] + _rms(y_ref[...], gp_ref[...])

    @pl.when(i < last)
    def _():
        op_ref[...] = xo

    @pl.when(i == last)
    def _():
        os_ref[...] = xo


def _res_last(x, y, g_post, n_prompt):
    m, d = x.shape
    nb = m // CHUNK
    row = pl.BlockSpec((CHUNK, d), lambda i: (i, 0))
    vec = pl.BlockSpec((1, d), lambda i: (0, 0))
    return pl.pallas_call(
        _res_last_kernel, grid=(nb,), in_specs=[row, row, vec],
        out_specs=[pl.BlockSpec((CHUNK, d), lambda i: (jnp.minimum(i, nb - 2), 0)),
                   pl.BlockSpec((CHUNK, d), lambda i: (0, 0))],
        out_shape=[jax.ShapeDtypeStruct((n_prompt, d), F32),
                   jax.ShapeDtypeStruct((m - n_prompt, d), F32)],
        compiler_params=_params("arbitrary"), name="res_last",
    )(x, y, g_post.reshape(1, d))


def _cast_kernel(w_ref, o_ref):
    o_ref[...] = w_ref[...].astype(BF16)


def _cast(w, layer):
    _, r, c = w.shape
    return pl.pallas_call(
        _cast_kernel,
        grid=(r // TC,),
        in_specs=[pl.BlockSpec((None, TC, c), lambda i: (layer, i, 0))],
        out_specs=pl.BlockSpec((TC, c), lambda i: (i, 0)),
        out_shape=jax.ShapeDtypeStruct((r, c), BF16),
        compiler_params=_params("parallel"),
        name="cast",
    )(w)


def _proj_kernel(a_ref, w_ref, o_ref):
    o_ref[...] = jnp.dot(a_ref[...], w_ref[...], preferred_element_type=F32)


def _proj(a, w, name):
    m, k = a.shape
    n = w.shape[1]
    return pl.pallas_call(
        _proj_kernel,
        grid=(m // TM, n // TN),
        in_specs=[pl.BlockSpec((TM, k), lambda i, j: (i, 0)),
                  pl.BlockSpec((k, TN), lambda i, j: (0, j))],
        out_specs=pl.BlockSpec((TM, TN), lambda i, j: (i, j)),
        out_shape=jax.ShapeDtypeStruct((m, n), F32),
        compiler_params=_params("parallel", "arbitrary"),
        name=name,
    )(a, w)


def _even_proj_kernel(xn_ref, wu_ref, wv_ref, wx_ref, wc_ref, wb_ref,
                      u_ref, v_ref, z_ref, gb_ref):
    x = xn_ref[...]

    def dot(w_ref):
        return jnp.dot(x, w_ref[...], preferred_element_type=F32)

    u_ref[...] = jax.nn.gelu(dot(wu_ref), approximate=True)
    v_ref[...] = jax.nn.gelu(dot(wv_ref), approximate=True)
    z_ref[...] = dot(wc_ref) * dot(wx_ref)
    gb_ref[...] = dot(wb_ref)


def _even_proj(xn, w_in):
    m, d = xn.shape
    width = w_in.shape[1] // 5
    nt = width // TN_EVEN

    def wspec(group):
        return pl.BlockSpec((d, TN_EVEN), lambda i, j: (0, group * nt + j))

    out = pl.BlockSpec((TM, TN_EVEN), lambda i, j: (i, j))
    return pl.pallas_call(
        _even_proj_kernel,
        grid=(m // TM, nt),
        in_specs=[pl.BlockSpec((TM, d), lambda i, j: (i, 0)),
                  wspec(0), wspec(1), wspec(2), wspec(3), wspec(4)],
        out_specs=[out] * 4,
        out_shape=[jax.ShapeDtypeStruct((m, width), F32)] * 4,
        compiler_params=_params("parallel", "arbitrary"),
        name="even_proj",
    )(xn, w_in, w_in, w_in, w_in, w_in)


def _even_mix_kernel(u_ref, v_ref, z_ref, gb_ref, zh_ref, w_ref, b_ref, cw_ref, s_ref, ab_ref):
    c = pl.program_id(0)
    n_prompt = pl.num_programs(0) - 1
    hd = u_ref.shape[1] // A_HEADS
    half = u_ref.shape[1]

    v = v_ref[...].astype(BF16)
    for h in range(A_HEADS):
        cols = slice(h * hd, (h + 1) * hd)
        mixed = jnp.dot(w_ref[0, h], v[:, cols], preferred_element_type=F32)
        mixed = mixed + b_ref[0][:, h:h + 1]
        ab_ref[:, cols] = (u_ref[:, cols] * mixed).astype(BF16)

    z = z_ref[...]
    row = lax.broadcasted_iota(jnp.int32, (CHUNK, 1), 0)
    r1 = pltpu.roll(z, 1, 0)
    r2 = pltpu.roll(z, 2, 0)
    cw = cw_ref[...]

    def emit(zp1, zp2):
        conv = cw[0:1] * zp2 + cw[1:2] * zp1 + cw[2:3] * z
        ab_ref[:, half:] = (gb_ref[...] * conv).astype(BF16)

    @pl.when(c < n_prompt)
    def _():
        keep = c % SEQ_CHUNKS != 0
        h1 = jnp.where(keep, zh_ref[7:8, :], 0.0)
        h2 = jnp.where(keep, zh_ref[6:7, :], 0.0)
        emit(jnp.where(row >= 1, r1, h1),
             jnp.where(row >= 2, r2, jnp.where(row == 1, h1, h2)))

    @pl.when(c == n_prompt)
    def _():
        t = row % 4
        emit(jnp.where(t >= 1, r1, s_ref[0]), jnp.where(t >= 2, r2, s_ref[1]))


def _even_mix(u, v, z, gb, wmix, bmix, conv_w, state_rows):
    m, half = u.shape
    nchunks = m // CHUNK
    blk = pl.BlockSpec((CHUNK, half), lambda c: (c, 0))
    sel = lambda c: (c // (nchunks - 1), 0, 0, 0)
    return pl.pallas_call(
        _even_mix_kernel,
        grid=(nchunks,),
        in_specs=[blk, blk, blk, blk,
                  pl.BlockSpec((8, half), lambda c: (jnp.maximum(c * (CHUNK // 8) - 1, 0), 0)),
                  pl.BlockSpec((1, A_HEADS, CHUNK, CHUNK), sel),
                  pl.BlockSpec((1, CHUNK, A_HEADS), lambda c: (c // (nchunks - 1), 0, 0)),
                  pl.BlockSpec((3, half), lambda c: (0, 0)),
                  pl.BlockSpec((2, CHUNK, half), lambda c: (0, 0, 0))],
        out_specs=pl.BlockSpec((CHUNK, 2 * half), lambda c: (c, 0)),
        out_shape=jax.ShapeDtypeStruct((m, 2 * half), BF16),
        compiler_params=_params("arbitrary"),
        name="even_mix",
    )(u, v, z, gb, z, wmix, bmix, conv_w, state_rows)


def _attn_prompt_kernel(q_ref, kc_ref, vc_ref, kp_ref, vp_ref, bias_ref, sink_ref, os_ref, o_ref):
    i = pl.program_id(0)
    last = pl.num_programs(0) - 1

    @pl.when(i < last)
    def _():
        _attn_prompt_block(i, q_ref, kc_ref, vc_ref, kp_ref, vp_ref, bias_ref, sink_ref, o_ref)

    @pl.when(i == last)
    def _():
        o_ref[...] = os_ref[...]


def _attn_prompt_block(i, q_ref, kc_ref, vc_ref, kp_ref, vp_ref, bias_ref, sink_ref, o_ref):
    pair_w = 2 * HEAD_DIM
    kk = jnp.concatenate([kp_ref[...], kc_ref[...]], axis=0).astype(BF16)
    vt = jnp.concatenate([vp_ref[...], vc_ref[...]], axis=0).T.astype(BF16)
    key = lax.broadcasted_iota(jnp.int32, (2 * CHUNK, 2 * CHUNK), 0)
    qry = lax.broadcasted_iota(jnp.int32, (2 * CHUNK, 2 * CHUNK), 1) % CHUNK
    dist = qry + CHUNK - key
    has_prev = i % SEQ_CHUNKS != 0
    valid = (dist >= 0) & (dist < CHUNK) & (has_prev | (key >= CHUNK))
    for p in range(N_KV_HEADS * GQA_GROUP // 2):
        g = (2 * p) // GQA_GROUP
        gc = slice(g * HEAD_DIM, (g + 1) * HEAD_DIM)
        pc = slice(p * pair_w, (p + 1) * pair_w)
        qt = (q_ref[:, pc].T * (HEAD_DIM ** -0.5)).astype(BF16)
        qt = jnp.concatenate([qt[:HEAD_DIM], qt[HEAD_DIM:]], axis=1)
        s = jnp.dot(kk[:, gc], qt, preferred_element_type=F32) + bias_ref[p]
        s = jnp.where(valid, s, NEG_INF)
        sk = sink_ref[p]
        m = jnp.maximum(jnp.max(s, axis=0, keepdims=True), sk)
        e = jnp.exp(s - m)
        den = jnp.sum(e, axis=0, keepdims=True) + jnp.exp(sk - m)
        ot = jnp.dot(vt[gc, :], e.astype(BF16), preferred_element_type=F32) / den
        o = jnp.concatenate([ot[:, :CHUNK], ot[:, CHUNK:]], axis=0).T
        o_ref[:, pc] = o.astype(BF16)


def _attn_prompt(qkv, o_sample, bias_t, sink_rows):
    m = qkv.shape[0]
    nb = m // CHUNK
    dq = N_KV_HEADS * GQA_GROUP * HEAD_DIM
    dkv = N_KV_HEADS * HEAD_DIM
    kcol, vcol = dq // dkv, dq // dkv + 1
    cur = lambda i: jnp.minimum(i, nb - 2)
    prev = lambda i: jnp.maximum(cur(i) - 1, 0)
    return pl.pallas_call(
        _attn_prompt_kernel,
        grid=(nb,),
        in_specs=[pl.BlockSpec((CHUNK, dq), lambda i: (cur(i), 0)),
                  pl.BlockSpec((CHUNK, dkv), lambda i: (cur(i), kcol)),
                  pl.BlockSpec((CHUNK, dkv), lambda i: (cur(i), vcol)),
                  pl.BlockSpec((CHUNK, dkv), lambda i: (prev(i), kcol)),
                  pl.BlockSpec((CHUNK, dkv), lambda i: (prev(i), vcol)),
                  pl.BlockSpec(bias_t.shape, lambda i: (0, 0, 0)),
                  pl.BlockSpec(sink_rows.shape, lambda i: (0, 0, 0)),
                  pl.BlockSpec(o_sample.shape, lambda i: (0, 0))],
        out_specs=pl.BlockSpec((CHUNK, dq), lambda i: (i, 0)),
        out_shape=jax.ShapeDtypeStruct((m, dq), BF16),
        compiler_params=_params("arbitrary"),
        name="attn_prompt",
    )(qkv, qkv, qkv, qkv, qkv, bias_t, sink_rows, o_sample)


def _attn_sample_kernel(q_ref, kn_ref, vn_ref, kc_ref, vc_ref, bc_ref, bn_ref, sink_ref, o_ref):
    kc = kc_ref[0].astype(BF16)
    vc = vc_ref[0].astype(BF16)
    kn = kn_ref[0].astype(BF16)
    vn = vn_ref[0].astype(BF16)
    rows = q_ref.shape[2]
    dec = rows // GQA_GROUP
    t_c = lax.broadcasted_iota(jnp.int32, (rows, CHUNK), 0) % dec
    j_c = lax.broadcasted_iota(jnp.int32, (rows, CHUNK), 1)
    valid_c = j_c > t_c
    t_n = lax.broadcasted_iota(jnp.int32, (rows, PAD_NEW_KEYS), 0) % dec
    j_n = lax.broadcasted_iota(jnp.int32, (rows, PAD_NEW_KEYS), 1)
    valid_n = j_n <= t_n
    nt = (((1,), (1,)), ((), ()))
    for g in range(N_KV_HEADS):
        gc = slice(g * HEAD_DIM, (g + 1) * HEAD_DIM)
        q = (q_ref[0, g] * (HEAD_DIM ** -0.5)).astype(BF16)
        sc = lax.dot_general(q, kc[:, gc], nt, preferred_element_type=F32) + bc_ref[g]
        sn = lax.dot_general(q, kn[:, gc], nt, preferred_element_type=F32) + bn_ref[g]
        sc = jnp.where(valid_c, sc, NEG_INF)
        sn = jnp.where(valid_n, sn, NEG_INF)
        sk = sink_ref[g]
        m = jnp.maximum(jnp.maximum(jnp.max(sc, axis=-1, keepdims=True),
                                    jnp.max(sn, axis=-1, keepdims=True)), sk)
        pc = jnp.exp(sc - m)
        pn = jnp.exp(sn - m)
        den = (jnp.sum(pc, axis=-1, keepdims=True) + jnp.sum(pn, axis=-1, keepdims=True)
               + jnp.exp(sk - m))
        o = (jnp.dot(pc.astype(BF16), vc[:, gc], preferred_element_type=F32)
             + jnp.dot(pn.astype(BF16), vn[:, gc], preferred_element_type=F32))
        o_ref[0, g] = o / den


def _attn_sample(q, kn, vn, kc, vc, bias_c, bias_n, sink_rows):
    nb, _, rows, _ = q.shape
    dkv = N_KV_HEADS * HEAD_DIM
    full = lambda a: pl.BlockSpec(a.shape, lambda b: (0,) * a.ndim)
    return pl.pallas_call(
        _attn_sample_kernel,
        grid=(nb,),
        in_specs=[pl.BlockSpec((1, N_KV_HEADS, rows, HEAD_DIM), lambda b: (b, 0, 0, 0)),
                  pl.BlockSpec((1, PAD_NEW_KEYS, dkv), lambda b: (b, 0, 0)),
                  pl.BlockSpec((1, PAD_NEW_KEYS, dkv), lambda b: (b, 0, 0)),
                  pl.BlockSpec((1, CHUNK, dkv), lambda b: (b, 0, 0)),
                  pl.BlockSpec((1, CHUNK, dkv), lambda b: (b, 0, 0)),
                  full(bias_c), full(bias_n), full(sink_rows)],
        out_specs=pl.BlockSpec((1, N_KV_HEADS, rows, HEAD_DIM), lambda b: (b, 0, 0, 0)),
        out_shape=jax.ShapeDtypeStruct(q.shape, F32),
        compiler_params=_params("parallel"),
        name="attn_sample",
    )(q, kn, vn, kc, vc, bias_c, bias_n, sink_rows)


def _ffn_kernel(xn_ref, wg_ref, wu_ref, wd_ref, y_ref):
    f = pl.program_id(1)
    x = xn_ref[...]
    gate = jnp.dot(x, wg_ref[...], preferred_element_type=F32)
    up = jnp.dot(x, wu_ref[...], preferred_element_type=F32)
    h = (jax.nn.silu(gate) * up).astype(BF16)

    @pl.when(f == 0)
    def _():
        y_ref[...] = jnp.dot(h, wd_ref[...], preferred_element_type=F32)

    @pl.when(f > 0)
    def _():
        y_ref[...] += jnp.dot(h, wd_ref[...], preferred_element_type=F32)


def _ffn(xn, wg, wu, wd):
    m, d = xn.shape
    dff = wg.shape[1]
    return pl.pallas_call(
        _ffn_kernel,
        grid=(m // TM_FFN, dff // TF),
        in_specs=[pl.BlockSpec((TM_FFN, d), lambda i, f: (i, 0)),
                  pl.BlockSpec((d, TF), lambda i, f: (0, f)),
                  pl.BlockSpec((d, TF), lambda i, f: (0, f)),
                  pl.BlockSpec((TF, d), lambda i, f: (f, 0))],
        out_specs=pl.BlockSpec((TM_FFN, d), lambda i, f: (i, 0), pipeline_mode=pl.Buffered(1)),
        out_shape=jax.ShapeDtypeStruct((m, d), F32),
        compiler_params=_params("parallel", "arbitrary"),
        name="ffn",
    )(xn, wg, wu, wd)


def _t5_bucket(dist):
    max_exact = N_BUCKETS // 2
    d = jnp.maximum(dist, max_exact).astype(F32)
    large = max_exact + (jnp.log(d / max_exact) / math.log(MAX_DISTANCE / max_exact)
                         * (N_BUCKETS - max_exact)).astype(jnp.int32)
    return jnp.where(dist < max_exact, dist, jnp.minimum(large, N_BUCKETS - 1))


def _bias_tables(rel_bias, dec_seq):
    n_heads = rel_bias.shape[1]
    assert n_heads == N_KV_HEADS * GQA_GROUP
    by_dist = rel_bias.astype(F32)[_t5_bucket(jnp.arange(CHUNK))].T
    span = 3 * CHUNK
    row = by_dist[:, jnp.clip(jnp.arange(span) - (CHUNK - 1), 0, CHUNK - 1)]
    flat = jnp.broadcast_to(row[:, None, :], (n_heads, 2 * CHUNK, span)).reshape(n_heads, -1)
    start = 2 * CHUNK - 1
    bias_t = flat[:, start:start + 2 * CHUNK * (span - 1)].reshape(n_heads, 2 * CHUNK, span - 1)
    bias_t = bias_t[:, :, :CHUNK]
    bias_t = bias_t.reshape(n_heads // 2, 2, 2 * CHUNK, CHUNK).transpose(0, 2, 1, 3)
    bias_t = bias_t.reshape(n_heads // 2, 2 * CHUNK, 2 * CHUNK)
    t = jnp.arange(dec_seq)[:, None]
    dist_c = jnp.clip(t + CHUNK - jnp.arange(CHUNK)[None, :], 0, CHUNK - 1)
    dist_n = jnp.clip(t - jnp.arange(PAD_NEW_KEYS)[None, :], 0, CHUNK - 1)
    rows = GQA_GROUP * dec_seq
    bias_c = by_dist[:, dist_c].reshape(N_KV_HEADS, rows, CHUNK)
    bias_n = by_dist[:, dist_n].reshape(N_KV_HEADS, rows, PAD_NEW_KEYS)
    return bias_t, bias_c, bias_n


def _mix_tables(w_s, b_s, dec_batch, dec_seq):
    assert dec_batch * dec_seq == CHUNK
    w_p = jnp.tril(w_s)
    small = jnp.tril(w_s[:, :dec_seq, :dec_seq])
    eye = jnp.eye(dec_batch, dtype=w_s.dtype)
    w_d = jnp.einsum("bc,hij->hbicj", eye, small).reshape(w_s.shape)
    b_p = b_s.T
    b_d = jnp.tile(b_s[:, :dec_seq].T, (dec_batch, 1))
    return jnp.stack([w_p, w_d]).astype(BF16), jnp.stack([b_p, b_d])


def kernel(x_prompt, x_sample, state_conv, cache_win_k, cache_win_v, norm_mix_pre, norm_mix_post, norm_ffn_pre, norm_ffn_post, w_in_even, w_out_even, sgu_w, sgu_b, conv_w, w_qkv_odd, w_o_odd, attn_sinks, rel_bias, ffn_w_gate, ffn_w_up, ffn_w_down):
    batch, seq, d = x_prompt.shape
    dec_batch, dec_seq, _ = x_sample.shape
    depth = norm_mix_pre.shape[0]
    n_p = batch * seq
    n_s = dec_batch * dec_seq
    dq = N_KV_HEADS * GQA_GROUP * HEAD_DIM
    dkv = N_KV_HEADS * HEAD_DIM
    assert seq == SEQ_CHUNKS * CHUNK and n_s == CHUNK and dec_seq <= PAD_NEW_KEYS

    x_p, x_s = x_prompt.reshape(n_p, d), x_sample.reshape(n_s, d)
    xn = _norm_first(x_p, x_s, norm_mix_pre[0])
    bias_t, bias_c, bias_n = _bias_tables(rel_bias, dec_seq)

    conv_p, conv_s, chunk_v_s = [], [], []
    win_kp, win_vp, win_ks, win_vs = [], [], [], []
    for layer in range(depth):
        i = layer // 2
        if layer % 2 == 0:
            u, v, z, gb = _even_proj(xn, _cast(w_in_even, i))
            wmix, bmix = _mix_tables(sgu_w[i], sgu_b[i], dec_batch, dec_seq)
            st = state_conv[i]
            zero = jnp.zeros_like(st[:, :1])
            s1 = jnp.concatenate([st[:, 1:2], zero, zero, zero], axis=1)
            s2 = jnp.concatenate([st[:, 0:1], st[:, 1:2], zero, zero], axis=1)
            state_rows = jnp.stack([s1.reshape(n_s, -1), s2.reshape(n_s, -1)])
            a = _even_mix(u, v, z, gb, wmix, bmix, conv_w[i], state_rows)
            y = _proj(a, _cast(w_out_even, i), "even_out")
            zc = z.shape[1]
            conv_p.append(jnp.stack([z[(b + 1) * seq - 2:(b + 1) * seq] for b in range(batch)]))
            conv_s.append(z[n_p:].reshape(dec_batch, dec_seq, zc)[:, dec_seq - 2:])
            chunk_v_s.append(v[n_p:].reshape(dec_batch, dec_seq, A_HEADS, zc // A_HEADS))
        else:
            qkv = _proj(xn, _cast(w_qkv_odd, i), "qkv")
            qs = qkv[n_p:, :dq].reshape(dec_batch, dec_seq, N_KV_HEADS, GQA_GROUP, HEAD_DIM)
            qs = qs.transpose(0, 2, 3, 1, 4).reshape(dec_batch, N_KV_HEADS, GQA_GROUP * dec_seq, HEAD_DIM)
            k_new = qkv[n_p:, dq:dq + dkv].reshape(dec_batch, dec_seq, dkv)
            v_new = qkv[n_p:, dq + dkv:].reshape(dec_batch, dec_seq, dkv)
            pad = ((0, 0), (0, PAD_NEW_KEYS - dec_seq), (0, 0))
            sink_rows = jnp.repeat(attn_sinks[i].astype(F32), dec_seq).reshape(N_KV_HEADS, GQA_GROUP * dec_seq, 1)
            o_s = _attn_sample(qs, jnp.pad(k_new, pad), jnp.pad(v_new, pad),
                               cache_win_k[i].reshape(dec_batch, -1, dkv),
                               cache_win_v[i].reshape(dec_batch, -1, dkv),
                               bias_c, bias_n, sink_rows)
            o_s = o_s.reshape(dec_batch, N_KV_HEADS, GQA_GROUP, dec_seq, HEAD_DIM)
            o_s = o_s.transpose(0, 3, 1, 2, 4).reshape(n_s, dq).astype(BF16)
            sink_pairs = jnp.repeat(attn_sinks[i].astype(F32), CHUNK).reshape(-1, 1, 2 * CHUNK)
            y = _proj(_attn_prompt(qkv, o_s, bias_t, sink_pairs), _cast(w_o_odd, i), "attn_out")
            win = CHUNK
            tail = lambda c0: jnp.stack([qkv[(b + 1) * seq - win:(b + 1) * seq, c0:c0 + dkv]
                                         for b in range(batch)]).reshape(batch, win, N_KV_HEADS, HEAD_DIM)
            win_kp.append(tail(dq))
            win_vp.append(tail(dq + dkv))
            k_all = jnp.concatenate([cache_win_k[i], k_new.reshape(dec_batch, dec_seq, N_KV_HEADS, HEAD_DIM)], axis=1)
            v_all = jnp.concatenate([cache_win_v[i], v_new.reshape(dec_batch, dec_seq, N_KV_HEADS, HEAD_DIM)], axis=1)
            win_ks.append(k_all[:, k_all.shape[1] - win:])
            win_vs.append(v_all[:, v_all.shape[1] - win:])
        if layer == 0:
            x, xn = _resnorm_first(x_p, x_s, y, norm_mix_post[layer], norm_ffn_pre[layer])
        else:
            x, xn = _resnorm(x, y, norm_mix_post[layer], norm_ffn_pre[layer])
        y = _ffn(xn, _cast(ffn_w_gate, layer), _cast(ffn_w_up, layer), _cast(ffn_w_down, layer))
        if layer + 1 < depth:
            x, xn = _resnorm(x, y, norm_ffn_post[layer], norm_mix_pre[layer + 1])
    out_p, out_s = _res_last(x, y, norm_ffn_post[depth - 1], n_p)

    return (out_p.reshape(batch, seq, d), out_s.reshape(dec_batch, dec_seq, d),
            jnp.stack(conv_p), jnp.stack(conv_s), jnp.stack(win_kp), jnp.stack(win_vp),
            jnp.stack(win_ks), jnp.stack(win_vs), jnp.stack(chunk_v_s))
```

```python
import functools
import math

import jax
import jax.numpy as jnp
from jax import lax
from jax.experimental import pallas as pl
from jax.experimental.pallas import tpu as pltpu

F32 = jnp.float32
BF16 = jnp.bfloat16

EPS = 1e-6
NEG_INF = -1e30
CHUNK = 128
A_HEADS = 8
N_KV_HEADS = 8
GQA_GROUP = 8
HEAD_DIM = 64
N_BUCKETS = 32
MAX_DISTANCE = 128
SEQ_CHUNKS = 16

VMEM_LIMIT_BYTES = 56 * 1024 * 1024

TM = 640
TR = 320
TN = 512
TN_EVEN = 256
TF = 256
TM_FFN = 1040
TC = 256
PAD_NEW_KEYS = 16


def _params(*sem):
    return pltpu.CompilerParams(dimension_semantics=sem, vmem_limit_bytes=VMEM_LIMIT_BYTES)


def _rms(x, g):
    return x * lax.rsqrt(jnp.mean(x * x, axis=-1, keepdims=True) + EPS) * g


def _two_source_specs(d, n_blocks):
    return [pl.BlockSpec((CHUNK, d), lambda i: (jnp.minimum(i, n_blocks - 2), 0)),
            pl.BlockSpec((CHUNK, d), lambda i: (0, 0))]


def _on_source(body, xp_ref, xs_ref):
    i = pl.program_id(0)
    last = pl.num_programs(0) - 1

    @pl.when(i < last)
    def _():
        body(xp_ref[...])

    @pl.when(i == last)
    def _():
        body(xs_ref[...])


def _norm_first_kernel(xp_ref, xs_ref, g_ref, xn_ref):
    def body(x):
        xn_ref[...] = _rms(x, g_ref[...]).astype(BF16)

    _on_source(body, xp_ref, xs_ref)


def _norm_first(xp, xs, g):
    d = xp.shape[1]
    m = xp.shape[0] + xs.shape[0]
    nb = m // CHUNK
    row = pl.BlockSpec((CHUNK, d), lambda i: (i, 0))
    vec = pl.BlockSpec((1, d), lambda i: (0, 0))
    return pl.pallas_call(
        _norm_first_kernel, grid=(nb,), in_specs=_two_source_specs(d, nb) + [vec], out_specs=row,
        out_shape=jax.ShapeDtypeStruct((m, d), BF16),
        compiler_params=_params("arbitrary"), name="norm_first",
    )(xp, xs, g.reshape(1, d))


def _resnorm_first_kernel(xp_ref, xs_ref, y_ref, gp_ref, gn_ref, xo_ref, xn_ref):
    def body(x):
        xo = x + _rms(y_ref[...], gp_ref[...])
        xo_ref[...] = xo
        xn_ref[...] = _rms(xo, gn_ref[...]).astype(BF16)

    _on_source(body, xp_ref, xs_ref)


def _resnorm_first(xp, xs, y, g_post, g_next):
    m, d = y.shape
    nb = m // CHUNK
    row = pl.BlockSpec((CHUNK, d), lambda i: (i, 0))
    vec = pl.BlockSpec((1, d), lambda i: (0, 0))
    return pl.pallas_call(
        _resnorm_first_kernel, grid=(nb,), in_specs=_two_source_specs(d, nb) + [row, vec, vec],
        out_specs=[row, row],
        out_shape=[jax.ShapeDtypeStruct((m, d), F32), jax.ShapeDtypeStruct((m, d), BF16)],
        compiler_params=_params("arbitrary"), name="resnorm_first",
    )(xp, xs, y, g_post.reshape(1, d), g_next.reshape(1, d))


def _resnorm_kernel(x_ref, y_ref, gp_ref, gn_ref, xo_ref, xn_ref):
    xo = x_ref[...] + _rms(y_ref[...], gp_ref[...])
    xo_ref[...] = xo
    xn_ref[...] = _rms(xo, gn_ref[...]).astype(BF16)


def _resnorm(x, y, g_post, g_next):
    m, d = x.shape
    row = pl.BlockSpec((TR, d), lambda i: (i, 0))
    vec = pl.BlockSpec((1, d), lambda i: (0, 0))
    return pl.pallas_call(
        _resnorm_kernel, grid=(m // TR,), in_specs=[row, row, vec, vec],
        out_specs=[row, row],
        out_shape=[jax.ShapeDtypeStruct((m, d), F32), jax.ShapeDtypeStruct((m, d), BF16)],
        compiler_params=_params("parallel"), name="resnorm",
    )(x, y, g_post.reshape(1, d), g_next.reshape(1, d))


def _res_last_kernel(x_ref, y_ref, gp_ref, op_ref, os_ref):
    i = pl.program_id(0)
    last = pl.num_programs(0) - 1
    xo = x_ref[...] + _rms(y_ref[...], gp_ref[...])

    @pl.when(i < last)
    def _():
        op_ref[...] = xo

    @pl.when(i == last)
    def _():
        os_ref[...] = xo


def _res_last(x, y, g_post, n_prompt):
    m, d = x.shape
    nb = m // CHUNK
    row = pl.BlockSpec((CHUNK, d), lambda i: (i, 0))
    vec = pl.BlockSpec((1, d), lambda i: (0, 0))
    return pl.pallas_call(
        _res_last_kernel, grid=(nb,), in_specs=[row, row, vec],
        out_specs=[pl.BlockSpec((CHUNK, d), lambda i: (jnp.minimum(i, nb - 2), 0)),
                   pl.BlockSpec((CHUNK, d), lambda i: (0, 0))],
        out_shape=[jax.ShapeDtypeStruct((n_prompt, d), F32),
                   jax.ShapeDtypeStruct((m - n_prompt, d), F32)],
        compiler_params=_params("arbitrary"), name="res_last",
    )(x, y, g_post.reshape(1, d))


def _cast_kernel(w_ref, o_ref):
    o_ref[...] = w_ref[...].astype(BF16)


def _cast(w, layer):
    _, r, c = w.shape
    return pl.pallas_call(
        _cast_kernel,
        grid=(r // TC,),
        in_specs=[pl.BlockSpec((None, TC, c), lambda i: (layer, i, 0))],
        out_specs=pl.BlockSpec((TC, c), lambda i: (i, 0)),
        out_shape=jax.ShapeDtypeStruct((r, c), BF16),
        compiler_params=_params("parallel"),
        name="cast",
    )(w)


def _proj_kernel(a_ref, w_ref, o_ref):
    o_ref[...] = jnp.dot(a_ref[...], w_ref[...], preferred_element_type=F32)


def _proj(a, w, name):
    m, k = a.shape
    n = w.shape[1]
    return pl.pallas_call(
        _proj_kernel,
        grid=(m // TM, n // TN),
        in_specs=[pl.BlockSpec((TM, k), lambda i, j: (i, 0)),
                  pl.BlockSpec((k, TN), lambda i, j: (0, j))],
        out_specs=pl.BlockSpec((TM, TN), lambda i, j: (i, j)),
        out_shape=jax.ShapeDtypeStruct((m, n), F32),
        compiler_params=_params("parallel", "arbitrary"),
        name=name,
    )(a, w)


def _even_proj_kernel(xn_ref, wu_ref, wv_ref, wx_ref, wc_ref, wb_ref,
                      u_ref, v_ref, z_ref, gb_ref):
    x = xn_ref[...]

    def dot(w_ref):
        return jnp.dot(x, w_ref[...], preferred_element_type=F32)

    u_ref[...] = jax.nn.gelu(dot(wu_ref), approximate=True)
    v_ref[...] = jax.nn.gelu(dot(wv_ref), approximate=True)
    z_ref[...] = dot(wc_ref) * dot(wx_ref)
    gb_ref[...] = dot(wb_ref)


def _even_proj(xn, w_in):
    m, d = xn.shape
    width = w_in.shape[1] // 5
    nt = width // TN_EVEN

    def wspec(group):
        return pl.BlockSpec((d, TN_EVEN), lambda i, j: (0, group * nt + j))

    out = pl.BlockSpec((TM, TN_EVEN), lambda i, j: (i, j))
    return pl.pallas_call(
        _even_proj_kernel,
        grid=(m // TM, nt),
        in_specs=[pl.BlockSpec((TM, d), lambda i, j: (i, 0)),
                  wspec(0), wspec(1), wspec(2), wspec(3), wspec(4)],
        out_specs=[out] * 4,
        out_shape=[jax.ShapeDtypeStruct((m, width), F32)] * 4,
        compiler_params=_params("parallel", "arbitrary"),
        name="even_proj",
    )(xn, w_in, w_in, w_in, w_in, w_in)


def _even_mix_kernel(u_ref, v_ref, z_ref, gb_ref, zh_ref, w_ref, b_ref, cw_ref, s_ref, ab_ref):
    c = pl.program_id(0)
    n_prompt = pl.num_programs(0) - 1
    hd = u_ref.shape[1] // A_HEADS
    half = u_ref.shape[1]

    v = v_ref[...].astype(BF16)
    for h in range(A_HEADS):
        cols = slice(h * hd, (h + 1) * hd)
        mixed = jnp.dot(w_ref[0, h], v[:, cols], preferred_element_type=F32)
        mixed = mixed + b_ref[0][:, h:h + 1]
        ab_ref[:, cols] = (u_ref[:, cols] * mixed).astype(BF16)

    z = z_ref[...]
    row = lax.broadcasted_iota(jnp.int32, (CHUNK, 1), 0)
    r1 = pltpu.roll(z, 1, 0)
    r2 = pltpu.roll(z, 2, 0)
    cw = cw_ref[...]

    def emit(zp1, zp2):
        conv = cw[0:1] * zp2 + cw[1:2] * zp1 + cw[2:3] * z
        ab_ref[:, half:] = (gb_ref[...] * conv).astype(BF16)

    @pl.when(c < n_prompt)
    def _():
        keep = c % SEQ_CHUNKS != 0
        h1 = jnp.where(keep, zh_ref[7:8, :], 0.0)
        h2 = jnp.where(keep, zh_ref[6:7, :], 0.0)
        emit(jnp.where(row >= 1, r1, h1),
             jnp.where(row >= 2, r2, jnp.where(row == 1, h1, h2)))

    @pl.when(c == n_prompt)
    def _():
        t = row % 4
        emit(jnp.where(t >= 1, r1, s_ref[0]), jnp.where(t >= 2, r2, s_ref[1]))


def _even_mix(u, v, z, gb, wmix, bmix, conv_w, state_rows):
    m, half = u.shape
    nchunks = m // CHUNK
    blk = pl.BlockSpec((CHUNK, half), lambda c: (c, 0))
    sel = lambda c: (c // (nchunks - 1), 0, 0, 0)
    return pl.pallas_call(
        _even_mix_kernel,
        grid=(nchunks,),
        in_specs=[blk, blk, blk, blk,
                  pl.BlockSpec((8, half), lambda c: (jnp.maximum(c * (CHUNK // 8) - 1, 0), 0)),
                  pl.BlockSpec((1, A_HEADS, CHUNK, CHUNK), sel),
                  pl.BlockSpec((1, CHUNK, A_HEADS), lambda c: (c // (nchunks - 1), 0, 0)),
                  pl.BlockSpec((3, half), lambda c: (0, 0)),
                  pl.BlockSpec((2, CHUNK, half), lambda c: (0, 0, 0))],
        out_specs=pl.BlockSpec((CHUNK, 2 * half), lambda c: (c, 0)),
        out_shape=jax.ShapeDtypeStruct((m, 2 * half), BF16),
        compiler_params=_params("arbitrary"),
        name="even_mix",
    )(u, v, z, gb, z, wmix, bmix, conv_w, state_rows)


def _attn_prompt_kernel(q_ref, kc_ref, vc_ref, kp_ref, vp_ref, bias_ref, sink_ref, os_ref, o_ref):
    i = pl.program_id(0)
    last = pl.num_programs(0) - 1

    @pl.when(i < last)
    def _():
        _attn_prompt_block(i, q_ref, kc_ref, vc_ref, kp_ref, vp_ref, bias_ref, sink_ref, o_ref)

    @pl.when(i == last)
    def _():
        o_ref[...] = os_ref[...]


def _attn_prompt_block(i, q_ref, kc_ref, vc_ref, kp_ref, vp_ref, bias_ref, sink_ref, o_ref):
    pair_w = 2 * HEAD_DIM
    kk = jnp.concatenate([kp_ref[...], kc_ref[...]], axis=0).astype(BF16)
    vt = jnp.concatenate([vp_ref[...], vc_ref[...]], axis=0).T.astype(BF16)
    key = lax.broadcasted_iota(jnp.int32, (2 * CHUNK, 2 * CHUNK), 0)
    qry = lax.broadcasted_iota(jnp.int32, (2 * CHUNK, 2 * CHUNK), 1) % CHUNK
    dist = qry + CHUNK - key
    has_prev = i % SEQ_CHUNKS != 0
    valid = (dist >= 0) & (dist < CHUNK) & (has_prev | (key >= CHUNK))
    for p in range(N_KV_HEADS * GQA_GROUP // 2):
        g = (2 * p) // GQA_GROUP
        gc = slice(g * HEAD_DIM, (g + 1) * HEAD_DIM)
        pc = slice(p * pair_w, (p + 1) * pair_w)
        qt = (q_ref[:, pc].T * (HEAD_DIM ** -0.5)).astype(BF16)
        qt = jnp.concatenate([qt[:HEAD_DIM], qt[HEAD_DIM:]], axis=1)
        s = jnp.dot(kk[:, gc], qt, preferred_element_type=F32) + bias_ref[p]
        s = jnp.where(valid, s, NEG_INF)
        sk = sink_ref[p]
        m = jnp.maximum(jnp.max(s, axis=0, keepdims=True), sk)
        e = jnp.exp(s - m)
        den = jnp.sum(e, axis=0, keepdims=True) + jnp.exp(sk - m)
        ot = jnp.dot(vt[gc, :], e.astype(BF16), preferred_element_type=F32) / den
        o = jnp.concatenate([ot[:, :CHUNK], ot[:, CHUNK:]], axis=0).T
        o_ref[:, pc] = o.astype(BF16)


def _attn_prompt(qkv, o_sample, bias_t, sink_rows):
    m = qkv.shape[0]
    nb = m // CHUNK
    dq = N_KV_HEADS * GQA_GROUP * HEAD_DIM
    dkv = N_KV_HEADS * HEAD_DIM
    kcol, vcol = dq // dkv, dq // dkv + 1
    cur = lambda i: jnp.minimum(i, nb - 2)
    prev = lambda i: jnp.maximum(cur(i) - 1, 0)
    return pl.pallas_call(
        _attn_prompt_kernel,
        grid=(nb,),
        in_specs=[pl.BlockSpec((CHUNK, dq), lambda i: (cur(i), 0)),
                  pl.BlockSpec((CHUNK, dkv), lambda i: (cur(i), kcol)),
                  pl.BlockSpec((CHUNK, dkv), lambda i: (cur(i), vcol)),
                  pl.BlockSpec((CHUNK, dkv), lambda i: (prev(i), kcol)),
                  pl.BlockSpec((CHUNK, dkv), lambda i: (prev(i), vcol)),
                  pl.BlockSpec(bias_t.shape, lambda i: (0, 0, 0)),
                  pl.BlockSpec(sink_rows.shape, lambda i: (0, 0, 0)),
                  pl.BlockSpec(o_sample.shape, lambda i: (0, 0))],
        out_specs=pl.BlockSpec((CHUNK, dq), lambda i: (i, 0)),
        out_shape=jax.ShapeDtypeStruct((m, dq), BF16),
        compiler_params=_params("arbitrary"),
        name="attn_prompt",
    )(qkv, qkv, qkv, qkv, qkv, bias_t, sink_rows, o_sample)


def _attn_sample_kernel(q_ref, kn_ref, vn_ref, kc_ref, vc_ref, bc_ref, bn_ref, sink_ref, o_ref):
    kc = kc_ref[0].astype(BF16)
    vc = vc_ref[0].astype(BF16)
    kn = kn_ref[0].astype(BF16)
    vn = vn_ref[0].astype(BF16)
    rows = q_ref.shape[2]
    dec = rows // GQA_GROUP
    t_c = lax.broadcasted_iota(jnp.int32, (rows, CHUNK), 0) % dec
    j_c = lax.broadcasted_iota(jnp.int32, (rows, CHUNK), 1)
    valid_c = j_c > t_c
    t_n = lax.broadcasted_iota(jnp.int32, (rows, PAD_NEW_KEYS), 0) % dec
    j_n = lax.broadcasted_iota(jnp.int32, (rows, PAD_NEW_KEYS), 1)
    valid_n = j_n <= t_n
    nt = (((1,), (1,)), ((), ()))
    for g in range(N_KV_HEADS):
        gc = slice(g * HEAD_DIM, (g + 1) * HEAD_DIM)
        q = (q_ref[0, g] * (HEAD_DIM ** -0.5)).astype(BF16)
        sc = lax.dot_general(q, kc[:, gc], nt, preferred_element_type=F32) + bc_ref[g]
        sn = lax.dot_general(q, kn[:, gc], nt, preferred_element_type=F32) + bn_ref[g]
        sc = jnp.where(valid_c, sc, NEG_INF)
        sn = jnp.where(valid_n, sn, NEG_INF)
        sk = sink_ref[g]
        m = jnp.maximum(jnp.maximum(jnp.max(sc, axis=-1, keepdims=True),
                                    jnp.max(sn, axis=-1, keepdims=True)), sk)
        pc = jnp.exp(sc - m)
        pn = jnp.exp(sn - m)
        den = (jnp.sum(pc, axis=-1, keepdims=True) + jnp.sum(pn, axis=-1, keepdims=True)
               + jnp.exp(sk - m))
        o = (jnp.dot(pc.astype(BF16), vc[:, gc], preferred_element_type=F32)
             + jnp.dot(pn.astype(BF16), vn[:, gc], preferred_element_type=F32))
        o_ref[0, g] = o / den


def _attn_sample(q, kn, vn, kc, vc, bias_c, bias_n, sink_rows):
    nb, _, rows, _ = q.shape
    dkv = N_KV_HEADS * HEAD_DIM
    full = lambda a: pl.BlockSpec(a.shape, lambda b: (0,) * a.ndim)
    return pl.pallas_call(
        _attn_sample_kernel,
        grid=(nb,),
        in_specs=[pl.BlockSpec((1, N_KV_HEADS, rows, HEAD_DIM), lambda b: (b, 0, 0, 0)),
                  pl.BlockSpec((1, PAD_NEW_KEYS, dkv), lambda b: (b, 0, 0)),
                  pl.BlockSpec((1, PAD_NEW_KEYS, dkv), lambda b: (b, 0, 0)),
                  pl.BlockSpec((1, CHUNK, dkv), lambda b: (b, 0, 0)),
                  pl.BlockSpec((1, CHUNK, dkv), lambda b: (b, 0, 0)),
                  full(bias_c), full(bias_n), full(sink_rows)],
        out_specs=pl.BlockSpec((1, N_KV_HEADS, rows, HEAD_DIM), lambda b: (b, 0, 0, 0)),
        out_shape=jax.ShapeDtypeStruct(q.shape, F32),
        compiler_params=_params("parallel"),
        name="attn_sample",
    )(q, kn, vn, kc, vc, bias_c, bias_n, sink_rows)


def _ffn_kernel(n_side, xn_ref, wg_ref, wu_ref, wd_ref, *refs):
    side_src, y_ref, side_dst = refs[:n_side], refs[n_side], refs[n_side + 1:]
    f = pl.program_id(1)
    x = xn_ref[...]
    gate = jnp.dot(x, wg_ref[...], preferred_element_type=F32)
    up = jnp.dot(x, wu_ref[...], preferred_element_type=F32)
    h = (jax.nn.silu(gate) * up).astype(BF16)

    def side_casts():
        for src, dst in zip(side_src, side_dst):
            dst[...] = src[...].astype(BF16)

    @pl.when(f == 0)
    def _():
        y_ref[...] = jnp.dot(h, wd_ref[...], preferred_element_type=F32)
        side_casts()

    @pl.when(f > 0)
    def _():
        y_ref[...] += jnp.dot(h, wd_ref[...], preferred_element_type=F32)
        side_casts()


def _side_cast_specs(w, layer, n_i, n_f):
    _, r, c = w.shape
    br = r // n_i
    assert br * n_i == r and br % 16 == 0
    n_c = max(n for n in range(1, n_f + 1) if c % n == 0 and (c // n) % 128 == 0)
    bc = c // n_c
    col = lambda f: jnp.minimum(f, n_c - 1)
    return (pl.BlockSpec((None, br, bc), lambda i, f: (layer, i, col(f))),
            pl.BlockSpec((br, bc), lambda i, f: (i, col(f))),
            jax.ShapeDtypeStruct((r, c), BF16))


def _ffn(xn, wg, wu, wd, side=()):
    m, d = xn.shape
    dff = wg.shape[1]
    n_i, n_f = m // TM_FFN, dff // TF
    specs = [_side_cast_specs(w, layer, n_i, n_f) for w, layer in side]
    outs = pl.pallas_call(
        functools.partial(_ffn_kernel, len(side)),
        grid=(n_i, n_f),
        in_specs=[pl.BlockSpec((TM_FFN, d), lambda i, f: (i, 0)),
                  pl.BlockSpec((d, TF), lambda i, f: (0, f)),
                  pl.BlockSpec((d, TF), lambda i, f: (0, f)),
                  pl.BlockSpec((TF, d), lambda i, f: (f, 0))] + [s[0] for s in specs],
        out_specs=[pl.BlockSpec((TM_FFN, d), lambda i, f: (i, 0), pipeline_mode=pl.Buffered(1))]
        + [s[1] for s in specs],
        out_shape=[jax.ShapeDtypeStruct((m, d), F32)] + [s[2] for s in specs],
        compiler_params=_params("arbitrary", "arbitrary"),
        name="ffn",
    )(xn, wg, wu, wd, *[w for w, _ in side])
    return outs[0], outs[1:]


def _t5_bucket(dist):
    max_exact = N_BUCKETS // 2
    d = jnp.maximum(dist, max_exact).astype(F32)
    large = max_exact + (jnp.log(d / max_exact) / math.log(MAX_DISTANCE / max_exact)
                         * (N_BUCKETS - max_exact)).astype(jnp.int32)
    return jnp.where(dist < max_exact, dist, jnp.minimum(large, N_BUCKETS - 1))


def _bias_tables(rel_bias, dec_seq):
    n_heads = rel_bias.shape[1]
    assert n_heads == N_KV_HEADS * GQA_GROUP
    by_dist = rel_bias.astype(F32)[_t5_bucket(jnp.arange(CHUNK))].T
    span = 3 * CHUNK
    row = by_dist[:, jnp.clip(jnp.arange(span) - (CHUNK - 1), 0, CHUNK - 1)]
    flat = jnp.broadcast_to(row[:, None, :], (n_heads, 2 * CHUNK, span)).reshape(n_heads, -1)
    start = 2 * CHUNK - 1
    bias_t = flat[:, start:start + 2 * CHUNK * (span - 1)].reshape(n_heads, 2 * CHUNK, span - 1)
    bias_t = bias_t[:, :, :CHUNK]
    bias_t = bias_t.reshape(n_heads // 2, 2, 2 * CHUNK, CHUNK).transpose(0, 2, 1, 3)
    bias_t = bias_t.reshape(n_heads // 2, 2 * CHUNK, 2 * CHUNK)
    t = jnp.arange(dec_seq)[:, None]
    dist_c = jnp.clip(t + CHUNK - jnp.arange(CHUNK)[None, :], 0, CHUNK - 1)
    dist_n = jnp.clip(t - jnp.arange(PAD_NEW_KEYS)[None, :], 0, CHUNK - 1)
    rows = GQA_GROUP * dec_seq
    bias_c = by_dist[:, dist_c].reshape(N_KV_HEADS, rows, CHUNK)
    bias_n = by_dist[:, dist_n].reshape(N_KV_HEADS, rows, PAD_NEW_KEYS)
    return bias_t, bias_c, bias_n


def _mix_tables(w_s, b_s, dec_batch, dec_seq):
    assert dec_batch * dec_seq == CHUNK
    w_p = jnp.tril(w_s)
    small = jnp.tril(w_s[:, :dec_seq, :dec_seq])
    eye = jnp.eye(dec_batch, dtype=w_s.dtype)
    w_d = jnp.einsum("bc,hij->hbicj", eye, small).reshape(w_s.shape)
    b_p = b_s.T
    b_d = jnp.tile(b_s[:, :dec_seq].T, (dec_batch, 1))
    return jnp.stack([w_p, w_d]).astype(BF16), jnp.stack([b_p, b_d])


def kernel(x_prompt, x_sample, state_conv, cache_win_k, cache_win_v, norm_mix_pre, norm_mix_post, norm_ffn_pre, norm_ffn_post, w_in_even, w_out_even, sgu_w, sgu_b, conv_w, w_qkv_odd, w_o_odd, attn_sinks, rel_bias, ffn_w_gate, ffn_w_up, ffn_w_down):
    batch, seq, d = x_prompt.shape
    dec_batch, dec_seq, _ = x_sample.shape
    depth = norm_mix_pre.shape[0]
    n_p = batch * seq
    n_s = dec_batch * dec_seq
    dq = N_KV_HEADS * GQA_GROUP * HEAD_DIM
    dkv = N_KV_HEADS * HEAD_DIM
    assert seq == SEQ_CHUNKS * CHUNK and n_s == CHUNK and dec_seq <= PAD_NEW_KEYS

    x_p, x_s = x_prompt.reshape(n_p, d), x_sample.reshape(n_s, d)
    xn = _norm_first(x_p, x_s, norm_mix_pre[0])
    bias_t, bias_c, bias_n = _bias_tables(rel_bias, dec_seq)

    def layer_weights(layer):
        mixer = ([(w_in_even, layer // 2), (w_out_even, layer // 2)] if layer % 2 == 0
                 else [(w_qkv_odd, layer // 2), (w_o_odd, layer // 2)])
        return mixer + [(ffn_w_gate, layer), (ffn_w_up, layer), (ffn_w_down, layer)]

    w_bf = [_cast(w, k) for w, k in layer_weights(0)]

    conv_p, conv_s, chunk_v_s = [], [], []
    win_kp, win_vp, win_ks, win_vs = [], [], [], []
    for layer in range(depth):
        i = layer // 2
        w_mix_in, w_mix_out, w_gate, w_up, w_down = w_bf
        if layer % 2 == 0:
            u, v, z, gb = _even_proj(xn, w_mix_in)
            wmix, bmix = _mix_tables(sgu_w[i], sgu_b[i], dec_batch, dec_seq)
            st = state_conv[i]
            zero = jnp.zeros_like(st[:, :1])
            s1 = jnp.concatenate([st[:, 1:2], zero, zero, zero], axis=1)
            s2 = jnp.concatenate([st[:, 0:1], st[:, 1:2], zero, zero], axis=1)
            state_rows = jnp.stack([s1.reshape(n_s, -1), s2.reshape(n_s, -1)])
            a = _even_mix(u, v, z, gb, wmix, bmix, conv_w[i], state_rows)
            y = _proj(a, w_mix_out, "even_out")
            zc = z.shape[1]
            conv_p.append(jnp.stack([z[(b + 1) * seq - 2:(b + 1) * seq] for b in range(batch)]))
            conv_s.append(z[n_p:].reshape(dec_batch, dec_seq, zc)[:, dec_seq - 2:])
            chunk_v_s.append(v[n_p:].reshape(dec_batch, dec_seq, A_HEADS, zc // A_HEADS))
        else:
            qkv = _proj(xn, w_mix_in, "qkv")
            qs = qkv[n_p:, :dq].reshape(dec_batch, dec_seq, N_KV_HEADS, GQA_GROUP, HEAD_DIM)
            qs = qs.transpose(0, 2, 3, 1, 4).reshape(dec_batch, N_KV_HEADS, GQA_GROUP * dec_seq, HEAD_DIM)
            k_new = qkv[n_p:, dq:dq + dkv].reshape(dec_batch, dec_seq, dkv)
            v_new = qkv[n_p:, dq + dkv:].reshape(dec_batch, dec_seq, dkv)
            pad = ((0, 0), (0, PAD_NEW_KEYS - dec_seq), (0, 0))
            sink_rows = jnp.repeat(attn_sinks[i].astype(F32), dec_seq).reshape(N_KV_HEADS, GQA_GROUP * dec_seq, 1)
            o_s = _attn_sample(qs, jnp.pad(k_new, pad), jnp.pad(v_new, pad),
                               cache_win_k[i].reshape(dec_batch, -1, dkv),
                               cache_win_v[i].reshape(dec_batch, -1, dkv),
                               bias_c, bias_n, sink_rows)
            o_s = o_s.reshape(dec_batch, N_KV_HEADS, GQA_GROUP, dec_seq, HEAD_DIM)
            o_s = o_s.transpose(0, 3, 1, 2, 4).reshape(n_s, dq).astype(BF16)
            sink_pairs = jnp.repeat(attn_sinks[i].astype(F32), CHUNK).reshape(-1, 1, 2 * CHUNK)
            y = _proj(_attn_prompt(qkv, o_s, bias_t, sink_pairs), w_mix_out, "attn_out")
            win = CHUNK
            tail = lambda c0: jnp.stack([qkv[(b + 1) * seq - win:(b + 1) * seq, c0:c0 + dkv]
                                         for b in range(batch)]).reshape(batch, win, N_KV_HEADS, HEAD_DIM)
            win_kp.append(tail(dq))
            win_vp.append(tail(dq + dkv))
            k_all = jnp.concatenate([cache_win_k[i], k_new.reshape(dec_batch, dec_seq, N_KV_HEADS, HEAD_DIM)], axis=1)
            v_all = jnp.concatenate([cache_win_v[i], v_new.reshape(dec_batch, dec_seq, N_KV_HEADS, HEAD_DIM)], axis=1)
            win_ks.append(k_all[:, k_all.shape[1] - win:])
            win_vs.append(v_all[:, v_all.shape[1] - win:])
        if layer == 0:
            x, xn = _resnorm_first(x_p, x_s, y, norm_mix_post[layer], norm_ffn_pre[layer])
        else:
            x, xn = _resnorm(x, y, norm_mix_post[layer], norm_ffn_pre[layer])
        y, w_bf = _ffn(xn, w_gate, w_up, w_down, layer_weights(layer + 1) if layer + 1 < depth else ())
        if layer + 1 < depth:
            x, xn = _resnorm(x, y, norm_ffn_post[layer], norm_mix_pre[layer + 1])
    out_p, out_s = _res_last(x, y, norm_ffn_post[depth - 1], n_p)

    return (out_p.reshape(batch, seq, d), out_s.reshape(dec_batch, dec_seq, d),
            jnp.stack(conv_p), jnp.stack(conv_s), jnp.stack(win_kp), jnp.stack(win_vp),
            jnp.stack(win_ks), jnp.stack(win_vs), jnp.stack(chunk_v_s))
```

```python
import functools
import math

import jax
import jax.numpy as jnp
from jax import lax
from jax.experimental import pallas as pl
from jax.experimental.pallas import tpu as pltpu

F32 = jnp.float32
BF16 = jnp.bfloat16

EPS = 1e-6
NEG_INF = -1e30
CHUNK = 128
A_HEADS = 8
N_KV_HEADS = 8
GQA_GROUP = 8
HEAD_DIM = 64
N_BUCKETS = 32
MAX_DISTANCE = 128
SEQ_CHUNKS = 16

VMEM_LIMIT_BYTES = 56 * 1024 * 1024

TM = 1040
TR = 320
TN = 512
TN_EVEN = 256
TF = 256
TC = 256
ATTN_HEADS = 8
PAD_NEW_KEYS = 16


def _params(*sem):
    return pltpu.CompilerParams(dimension_semantics=sem, vmem_limit_bytes=VMEM_LIMIT_BYTES)


def _rms(x, g):
    return x * lax.rsqrt(jnp.mean(x * x, axis=-1, keepdims=True) + EPS) * g


def _two_source_specs(d, n_blocks):
    return [pl.BlockSpec((CHUNK, d), lambda i: (jnp.minimum(i, n_blocks - 2), 0)),
            pl.BlockSpec((CHUNK, d), lambda i: (0, 0))]


def _on_source(body, xp_ref, xs_ref):
    i = pl.program_id(0)
    last = pl.num_programs(0) - 1

    @pl.when(i < last)
    def _():
        body(xp_ref[...])

    @pl.when(i == last)
    def _():
        body(xs_ref[...])


def _norm_first_kernel(xp_ref, xs_ref, g_ref, xn_ref):
    def body(x):
        xn_ref[...] = _rms(x, g_ref[...]).astype(BF16)

    _on_source(body, xp_ref, xs_ref)


def _norm_first(xp, xs, g):
    d = xp.shape[1]
    m = xp.shape[0] + xs.shape[0]
    nb = m // CHUNK
    row = pl.BlockSpec((CHUNK, d), lambda i: (i, 0))
    vec = pl.BlockSpec((1, d), lambda i: (0, 0))
    return pl.pallas_call(
        _norm_first_kernel, grid=(nb,), in_specs=_two_source_specs(d, nb) + [vec], out_specs=row,
        out_shape=jax.ShapeDtypeStruct((m, d), BF16),
        compiler_params=_params("arbitrary"), name="norm_first",
    )(xp, xs, g.reshape(1, d))


def _resnorm_first_kernel(xp_ref, xs_ref, y_ref, gp_ref, gn_ref, xo_ref, xn_ref):
    def body(x):
        xo = x + _rms(y_ref[...], gp_ref[...])
        xo_ref[...] = xo
        xn_ref[...] = _rms(xo, gn_ref[...]).astype(BF16)

    _on_source(body, xp_ref, xs_ref)


def _resnorm_first(xp, xs, y, g_post, g_next):
    m, d = y.shape
    nb = m // CHUNK
    row = pl.BlockSpec((CHUNK, d), lambda i: (i, 0))
    vec = pl.BlockSpec((1, d), lambda i: (0, 0))
    return pl.pallas_call(
        _resnorm_first_kernel, grid=(nb,), in_specs=_two_source_specs(d, nb) + [row, vec, vec],
        out_specs=[row, row],
        out_shape=[jax.ShapeDtypeStruct((m, d), F32), jax.ShapeDtypeStruct((m, d), BF16)],
        compiler_params=_params("arbitrary"), name="resnorm_first",
    )(xp, xs, y, g_post.reshape(1, d), g_next.reshape(1, d))


def _resnorm_kernel(x_ref, y_ref, gp_ref, gn_ref, xo_ref, xn_ref):
    xo = x_ref[...] + _rms(y_ref[...], gp_ref[...])
    xo_ref[...] = xo
    xn_ref[...] = _rms(xo, gn_ref[...]).astype(BF16)


def _resnorm(x, y, g_post, g_next):
    m, d = x.shape
    row = pl.BlockSpec((TR, d), lambda i: (i, 0))
    vec = pl.BlockSpec((1, d), lambda i: (0, 0))
    return pl.pallas_call(
        _resnorm_kernel, grid=(m // TR,), in_specs=[row, row, vec, vec],
        out_specs=[row, row],
        out_shape=[jax.ShapeDtypeStruct((m, d), F32), jax.ShapeDtypeStruct((m, d), BF16)],
        compiler_params=_params("parallel"), name="resnorm",
    )(x, y, g_post.reshape(1, d), g_next.reshape(1, d))


def _res_last_kernel(x_ref, y_ref, gp_ref, op_ref, os_ref):
    i = pl.program_id(0)
    last = pl.num_programs(0) - 1
    xo = x_ref[...] + _rms(y_ref[...], gp_ref[...])

    @pl.when(i < last)
    def _():
        op_ref[...] = xo

    @pl.when(i == last)
    def _():
        os_ref[...] = xo


def _res_last(x, y, g_post, n_prompt):
    m, d = x.shape
    nb = m // CHUNK
    row = pl.BlockSpec((CHUNK, d), lambda i: (i, 0))
    vec = pl.BlockSpec((1, d), lambda i: (0, 0))
    return pl.pallas_call(
        _res_last_kernel, grid=(nb,), in_specs=[row, row, vec],
        out_specs=[pl.BlockSpec((CHUNK, d), lambda i: (jnp.minimum(i, nb - 2), 0)),
                   pl.BlockSpec((CHUNK, d), lambda i: (0, 0))],
        out_shape=[jax.ShapeDtypeStruct((n_prompt, d), F32),
                   jax.ShapeDtypeStruct((m - n_prompt, d), F32)],
        compiler_params=_params("arbitrary"), name="res_last",
    )(x, y, g_post.reshape(1, d))


def _cast_kernel(w_ref, o_ref):
    o_ref[...] = w_ref[...].astype(BF16)


def _cast(w, layer):
    _, r, c = w.shape
    return pl.pallas_call(
        _cast_kernel,
        grid=(r // TC,),
        in_specs=[pl.BlockSpec((None, TC, c), lambda i: (layer, i, 0))],
        out_specs=pl.BlockSpec((TC, c), lambda i: (i, 0)),
        out_shape=jax.ShapeDtypeStruct((r, c), BF16),
        compiler_params=_params("parallel"),
        name="cast",
    )(w)


def _proj_kernel(a_ref, w_ref, o_ref):
    o_ref[...] = jnp.dot(a_ref[...], w_ref[...], preferred_element_type=F32)


def _proj(a, w, name):
    m, k = a.shape
    n = w.shape[1]
    return pl.pallas_call(
        _proj_kernel,
        grid=(m // TM, n // TN),
        in_specs=[pl.BlockSpec((TM, k), lambda i, j: (i, 0)),
                  pl.BlockSpec((k, TN), lambda i, j: (0, j))],
        out_specs=pl.BlockSpec((TM, TN), lambda i, j: (i, j)),
        out_shape=jax.ShapeDtypeStruct((m, n), F32),
        compiler_params=_params("parallel", "arbitrary"),
        name=name,
    )(a, w)


def _even_proj_kernel(xn_ref, wu_ref, wv_ref, wx_ref, wc_ref, wb_ref,
                      u_ref, v_ref, z_ref, gb_ref):
    x = xn_ref[...]

    def dot(w_ref):
        return jnp.dot(x, w_ref[...], preferred_element_type=F32)

    u_ref[...] = jax.nn.gelu(dot(wu_ref), approximate=True)
    v_ref[...] = jax.nn.gelu(dot(wv_ref), approximate=True)
    z_ref[...] = dot(wc_ref) * dot(wx_ref)
    gb_ref[...] = dot(wb_ref)


def _even_proj(xn, w_in):
    m, d = xn.shape
    width = w_in.shape[1] // 5
    nt = width // TN_EVEN

    def wspec(group):
        return pl.BlockSpec((d, TN_EVEN), lambda i, j: (0, group * nt + j))

    out = pl.BlockSpec((TM, TN_EVEN), lambda i, j: (i, j))
    return pl.pallas_call(
        _even_proj_kernel,
        grid=(m // TM, nt),
        in_specs=[pl.BlockSpec((TM, d), lambda i, j: (i, 0)),
                  wspec(0), wspec(1), wspec(2), wspec(3), wspec(4)],
        out_specs=[out] * 4,
        out_shape=[jax.ShapeDtypeStruct((m, width), F32)] * 4,
        compiler_params=_params("parallel", "arbitrary"),
        name="even_proj",
    )(xn, w_in, w_in, w_in, w_in, w_in)


def _even_mix_kernel(u_ref, v_ref, z_ref, gb_ref, zh_ref, w_ref, b_ref, cw_ref, s_ref, ab_ref):
    c = pl.program_id(0)
    n_prompt = pl.num_programs(0) - 1
    hd = u_ref.shape[1] // A_HEADS
    half = u_ref.shape[1]

    v = v_ref[...].astype(BF16)
    for h in range(A_HEADS):
        cols = slice(h * hd, (h + 1) * hd)
        mixed = jnp.dot(w_ref[0, h], v[:, cols], preferred_element_type=F32)
        mixed = mixed + b_ref[0][:, h:h + 1]
        ab_ref[:, cols] = (u_ref[:, cols] * mixed).astype(BF16)

    z = z_ref[...]
    row = lax.broadcasted_iota(jnp.int32, (CHUNK, 1), 0)
    r1 = pltpu.roll(z, 1, 0)
    r2 = pltpu.roll(z, 2, 0)
    cw = cw_ref[...]

    def emit(zp1, zp2):
        conv = cw[0:1] * zp2 + cw[1:2] * zp1 + cw[2:3] * z
        ab_ref[:, half:] = (gb_ref[...] * conv).astype(BF16)

    @pl.when(c < n_prompt)
    def _():
        keep = c % SEQ_CHUNKS != 0
        h1 = jnp.where(keep, zh_ref[7:8, :], 0.0)
        h2 = jnp.where(keep, zh_ref[6:7, :], 0.0)
        emit(jnp.where(row >= 1, r1, h1),
             jnp.where(row >= 2, r2, jnp.where(row == 1, h1, h2)))

    @pl.when(c == n_prompt)
    def _():
        t = row % 4
        emit(jnp.where(t >= 1, r1, s_ref[0]), jnp.where(t >= 2, r2, s_ref[1]))


def _even_mix(u, v, z, gb, wmix, bmix, conv_w, state_rows):
    m, half = u.shape
    nchunks = m // CHUNK
    blk = pl.BlockSpec((CHUNK, half), lambda c: (c, 0))
    sel = lambda c: (c // (nchunks - 1), 0, 0, 0)
    return pl.pallas_call(
        _even_mix_kernel,
        grid=(nchunks,),
        in_specs=[blk, blk, blk, blk,
                  pl.BlockSpec((8, half), lambda c: (jnp.maximum(c * (CHUNK // 8) - 1, 0), 0)),
                  pl.BlockSpec((1, A_HEADS, CHUNK, CHUNK), sel),
                  pl.BlockSpec((1, CHUNK, A_HEADS), lambda c: (c // (nchunks - 1), 0, 0)),
                  pl.BlockSpec((3, half), lambda c: (0, 0)),
                  pl.BlockSpec((2, CHUNK, half), lambda c: (0, 0, 0))],
        out_specs=pl.BlockSpec((CHUNK, 2 * half), lambda c: (c, 0)),
        out_shape=jax.ShapeDtypeStruct((m, 2 * half), BF16),
        compiler_params=_params("arbitrary"),
        name="even_mix",
    )(u, v, z, gb, z, wmix, bmix, conv_w, state_rows)


def _attn_prompt_kernel(q_ref, kc_ref, vc_ref, kp_ref, vp_ref, bias_ref, sink_ref, os_ref, o_ref):
    i = pl.program_id(0)
    last = pl.num_programs(0) - 1

    @pl.when(i < last)
    def _():
        _attn_prompt_block(i, q_ref, kc_ref, vc_ref, kp_ref, vp_ref, bias_ref, sink_ref, o_ref)

    @pl.when(i == last)
    def _():
        o_ref[...] = os_ref[...]


def _attn_prompt_block(i, q_ref, kc_ref, vc_ref, kp_ref, vp_ref, bias_ref, sink_ref, o_ref):
    nh = ATTN_HEADS
    width, lanes = nh * HEAD_DIM, nh * CHUNK
    kk = jnp.concatenate([kp_ref[...], kc_ref[...]], axis=0).astype(BF16)
    vt = jnp.concatenate([vp_ref[...], vc_ref[...]], axis=0).T.astype(BF16)
    key = lax.broadcasted_iota(jnp.int32, (2 * CHUNK, lanes), 0)
    qry = lax.broadcasted_iota(jnp.int32, (2 * CHUNK, lanes), 1) % CHUNK
    dist = qry + CHUNK - key
    has_prev = i % SEQ_CHUNKS != 0
    valid = (dist >= 0) & (dist < CHUNK) & (has_prev | (key >= CHUNK))
    for b in range(N_KV_HEADS * GQA_GROUP // nh):
        g = (b * nh) // GQA_GROUP
        gc = slice(g * HEAD_DIM, (g + 1) * HEAD_DIM)
        qt = (q_ref[:, b * width:(b + 1) * width].T * (HEAD_DIM ** -0.5)).astype(BF16)
        qt = jnp.concatenate([qt[h * HEAD_DIM:(h + 1) * HEAD_DIM] for h in range(nh)], axis=1)
        s = jnp.dot(kk[:, gc], qt, preferred_element_type=F32) + bias_ref[b]
        s = jnp.where(valid, s, NEG_INF)
        sk = sink_ref[b]
        m = jnp.maximum(jnp.max(s, axis=0, keepdims=True), sk)
        e = jnp.exp(s - m)
        den = jnp.sum(e, axis=0, keepdims=True) + jnp.exp(sk - m)
        ot = jnp.dot(vt[gc, :], e.astype(BF16), preferred_element_type=F32) / den
        for pair in range(nh // 2):
            lo = 2 * pair * CHUNK
            o = jnp.concatenate([ot[:, lo:lo + CHUNK], ot[:, lo + CHUNK:lo + 2 * CHUNK]], axis=0).T
            c0 = b * width + pair * 2 * HEAD_DIM
            o_ref[:, c0:c0 + 2 * HEAD_DIM] = o.astype(BF16)


def _attn_prompt(qkv, o_sample, bias_t, sink_rows):
    m = qkv.shape[0]
    nb = m // CHUNK
    dq = N_KV_HEADS * GQA_GROUP * HEAD_DIM
    dkv = N_KV_HEADS * HEAD_DIM
    kcol, vcol = dq // dkv, dq // dkv + 1
    cur = lambda i: jnp.minimum(i, nb - 2)
    prev = lambda i: jnp.maximum(cur(i) - 1, 0)
    return pl.pallas_call(
        _attn_prompt_kernel,
        grid=(nb,),
        in_specs=[pl.BlockSpec((CHUNK, dq), lambda i: (cur(i), 0)),
                  pl.BlockSpec((CHUNK, dkv), lambda i: (cur(i), kcol)),
                  pl.BlockSpec((CHUNK, dkv), lambda i: (cur(i), vcol)),
                  pl.BlockSpec((CHUNK, dkv), lambda i: (prev(i), kcol)),
                  pl.BlockSpec((CHUNK, dkv), lambda i: (prev(i), vcol)),
                  pl.BlockSpec(bias_t.shape, lambda i: (0, 0, 0)),
                  pl.BlockSpec(sink_rows.shape, lambda i: (0, 0, 0)),
                  pl.BlockSpec(o_sample.shape, lambda i: (0, 0))],
        out_specs=pl.BlockSpec((CHUNK, dq), lambda i: (i, 0)),
        out_shape=jax.ShapeDtypeStruct((m, dq), BF16),
        compiler_params=_params("arbitrary"),
        name="attn_prompt",
    )(qkv, qkv, qkv, qkv, qkv, bias_t, sink_rows, o_sample)


def _attn_sample_kernel(q_ref, kn_ref, vn_ref, kc_ref, vc_ref, bc_ref, bn_ref, sink_ref, o_ref):
    kc = kc_ref[0].astype(BF16)
    vc = vc_ref[0].astype(BF16)
    kn = kn_ref[0].astype(BF16)
    vn = vn_ref[0].astype(BF16)
    rows = q_ref.shape[2]
    dec = rows // GQA_GROUP
    t_c = lax.broadcasted_iota(jnp.int32, (rows, CHUNK), 0) % dec
    j_c = lax.broadcasted_iota(jnp.int32, (rows, CHUNK), 1)
    valid_c = j_c > t_c
    t_n = lax.broadcasted_iota(jnp.int32, (rows, PAD_NEW_KEYS), 0) % dec
    j_n = lax.broadcasted_iota(jnp.int32, (rows, PAD_NEW_KEYS), 1)
    valid_n = j_n <= t_n
    nt = (((1,), (1,)), ((), ()))
    for g in range(N_KV_HEADS):
        gc = slice(g * HEAD_DIM, (g + 1) * HEAD_DIM)
        q = (q_ref[0, g] * (HEAD_DIM ** -0.5)).astype(BF16)
        sc = lax.dot_general(q, kc[:, gc], nt, preferred_element_type=F32) + bc_ref[g]
        sn = lax.dot_general(q, kn[:, gc], nt, preferred_element_type=F32) + bn_ref[g]
        sc = jnp.where(valid_c, sc, NEG_INF)
        sn = jnp.where(valid_n, sn, NEG_INF)
        sk = sink_ref[g]
        m = jnp.maximum(jnp.maximum(jnp.max(sc, axis=-1, keepdims=True),
                                    jnp.max(sn, axis=-1, keepdims=True)), sk)
        pc = jnp.exp(sc - m)
        pn = jnp.exp(sn - m)
        den = (jnp.sum(pc, axis=-1, keepdims=True) + jnp.sum(pn, axis=-1, keepdims=True)
               + jnp.exp(sk - m))
        o = (jnp.dot(pc.astype(BF16), vc[:, gc], preferred_element_type=F32)
             + jnp.dot(pn.astype(BF16), vn[:, gc], preferred_element_type=F32))
        o_ref[0, g] = o / den


def _attn_sample(q, kn, vn, kc, vc, bias_c, bias_n, sink_rows):
    nb, _, rows, _ = q.shape
    dkv = N_KV_HEADS * HEAD_DIM
    full = lambda a: pl.BlockSpec(a.shape, lambda b: (0,) * a.ndim)
    return pl.pallas_call(
        _attn_sample_kernel,
        grid=(nb,),
        in_specs=[pl.BlockSpec((1, N_KV_HEADS, rows, HEAD_DIM), lambda b: (b, 0, 0, 0)),
                  pl.BlockSpec((1, PAD_NEW_KEYS, dkv), lambda b: (b, 0, 0)),
                  pl.BlockSpec((1, PAD_NEW_KEYS, dkv), lambda b: (b, 0, 0)),
                  pl.BlockSpec((1, CHUNK, dkv), lambda b: (b, 0, 0)),
                  pl.BlockSpec((1, CHUNK, dkv), lambda b: (b, 0, 0)),
                  full(bias_c), full(bias_n), full(sink_rows)],
        out_specs=pl.BlockSpec((1, N_KV_HEADS, rows, HEAD_DIM), lambda b: (b, 0, 0, 0)),
        out_shape=jax.ShapeDtypeStruct(q.shape, F32),
        compiler_params=_params("parallel"),
        name="attn_sample",
    )(q, kn, vn, kc, vc, bias_c, bias_n, sink_rows)


def _ffn_kernel(n_side, xn_ref, wg_ref, wu_ref, wd_ref, *refs):
    side_src, y_ref, side_dst = refs[:n_side], refs[n_side], refs[n_side + 1:]
    f = pl.program_id(1)
    x = xn_ref[...]
    gate = jnp.dot(x, wg_ref[...], preferred_element_type=F32)
    up = jnp.dot(x, wu_ref[...], preferred_element_type=F32)
    h = (jax.nn.silu(gate) * up).astype(BF16)

    def side_casts():
        for src, dst in zip(side_src, side_dst):
            dst[...] = src[...].astype(BF16)

    @pl.when(f == 0)
    def _():
        y_ref[...] = jnp.dot(h, wd_ref[...], preferred_element_type=F32)
        side_casts()

    @pl.when(f > 0)
    def _():
        y_ref[...] += jnp.dot(h, wd_ref[...], preferred_element_type=F32)
        side_casts()


def _side_cast_specs(w, layer, n_i, n_f):
    _, r, c = w.shape
    br = r // n_i
    assert br * n_i == r and br % 16 == 0
    n_c = max(n for n in range(1, n_f + 1) if c % n == 0 and (c // n) % 128 == 0)
    bc = c // n_c
    col = lambda f: jnp.minimum(f, n_c - 1)
    return (pl.BlockSpec((None, br, bc), lambda i, f: (layer, i, col(f))),
            pl.BlockSpec((br, bc), lambda i, f: (i, col(f))),
            jax.ShapeDtypeStruct((r, c), BF16))


def _ffn(xn, wg, wu, wd, side=()):
    m, d = xn.shape
    dff = wg.shape[1]
    n_i, n_f = m // TM, dff // TF
    specs = [_side_cast_specs(w, layer, n_i, n_f) for w, layer in side]
    outs = pl.pallas_call(
        functools.partial(_ffn_kernel, len(side)),
        grid=(n_i, n_f),
        in_specs=[pl.BlockSpec((TM, d), lambda i, f: (i, 0)),
                  pl.BlockSpec((d, TF), lambda i, f: (0, f)),
                  pl.BlockSpec((d, TF), lambda i, f: (0, f)),
                  pl.BlockSpec((TF, d), lambda i, f: (f, 0))] + [s[0] for s in specs],
        out_specs=[pl.BlockSpec((TM, d), lambda i, f: (i, 0), pipeline_mode=pl.Buffered(1))]
        + [s[1] for s in specs],
        out_shape=[jax.ShapeDtypeStruct((m, d), F32)] + [s[2] for s in specs],
        compiler_params=_params("arbitrary", "arbitrary"),
        name="ffn",
    )(xn, wg, wu, wd, *[w for w, _ in side])
    return outs[0], outs[1:]


def _t5_bucket(dist):
    max_exact = N_BUCKETS // 2
    d = jnp.maximum(dist, max_exact).astype(F32)
    large = max_exact + (jnp.log(d / max_exact) / math.log(MAX_DISTANCE / max_exact)
                         * (N_BUCKETS - max_exact)).astype(jnp.int32)
    return jnp.where(dist < max_exact, dist, jnp.minimum(large, N_BUCKETS - 1))


def _bias_tables(rel_bias, dec_seq):
    n_heads = rel_bias.shape[1]
    assert n_heads == N_KV_HEADS * GQA_GROUP
    by_dist = rel_bias.astype(F32)[_t5_bucket(jnp.arange(CHUNK))].T
    span = 3 * CHUNK
    row = by_dist[:, jnp.clip(jnp.arange(span) - (CHUNK - 1), 0, CHUNK - 1)]
    flat = jnp.broadcast_to(row[:, None, :], (n_heads, 2 * CHUNK, span)).reshape(n_heads, -1)
    start = 2 * CHUNK - 1
    bias_t = flat[:, start:start + 2 * CHUNK * (span - 1)].reshape(n_heads, 2 * CHUNK, span - 1)
    bias_t = bias_t[:, :, :CHUNK]
    nb = n_heads // ATTN_HEADS
    bias_t = bias_t.reshape(nb, ATTN_HEADS, 2 * CHUNK, CHUNK).transpose(0, 2, 1, 3)
    bias_t = bias_t.reshape(nb, 2 * CHUNK, ATTN_HEADS * CHUNK)
    t = jnp.arange(dec_seq)[:, None]
    dist_c = jnp.clip(t + CHUNK - jnp.arange(CHUNK)[None, :], 0, CHUNK - 1)
    dist_n = jnp.clip(t - jnp.arange(PAD_NEW_KEYS)[None, :], 0, CHUNK - 1)
    rows = GQA_GROUP * dec_seq
    bias_c = by_dist[:, dist_c].reshape(N_KV_HEADS, rows, CHUNK)
    bias_n = by_dist[:, dist_n].reshape(N_KV_HEADS, rows, PAD_NEW_KEYS)
    return bias_t, bias_c, bias_n


def _mix_tables(w_s, b_s, dec_batch, dec_seq):
    assert dec_batch * dec_seq == CHUNK
    w_p = jnp.tril(w_s)
    small = jnp.tril(w_s[:, :dec_seq, :dec_seq])
    eye = jnp.eye(dec_batch, dtype=w_s.dtype)
    w_d = jnp.einsum("bc,hij->hbicj", eye, small).reshape(w_s.shape)
    b_p = b_s.T
    b_d = jnp.tile(b_s[:, :dec_seq].T, (dec_batch, 1))
    return jnp.stack([w_p, w_d]).astype(BF16), jnp.stack([b_p, b_d])


def kernel(x_prompt, x_sample, state_conv, cache_win_k, cache_win_v, norm_mix_pre, norm_mix_post, norm_ffn_pre, norm_ffn_post, w_in_even, w_out_even, sgu_w, sgu_b, conv_w, w_qkv_odd, w_o_odd, attn_sinks, rel_bias, ffn_w_gate, ffn_w_up, ffn_w_down):
    batch, seq, d = x_prompt.shape
    dec_batch, dec_seq, _ = x_sample.shape
    depth = norm_mix_pre.shape[0]
    n_p = batch * seq
    n_s = dec_batch * dec_seq
    dq = N_KV_HEADS * GQA_GROUP * HEAD_DIM
    dkv = N_KV_HEADS * HEAD_DIM
    assert seq == SEQ_CHUNKS * CHUNK and n_s == CHUNK and dec_seq <= PAD_NEW_KEYS

    x_p, x_s = x_prompt.reshape(n_p, d), x_sample.reshape(n_s, d)
    xn = _norm_first(x_p, x_s, norm_mix_pre[0])
    bias_t, bias_c, bias_n = _bias_tables(rel_bias, dec_seq)

    def layer_weights(layer):
        mixer = ([(w_in_even, layer // 2), (w_out_even, layer // 2)] if layer % 2 == 0
                 else [(w_qkv_odd, layer // 2), (w_o_odd, layer // 2)])
        return mixer + [(ffn_w_gate, layer), (ffn_w_up, layer), (ffn_w_down, layer)]

    w_bf = [_cast(w, k) for w, k in layer_weights(0)]

    conv_p, conv_s, chunk_v_s = [], [], []
    win_kp, win_vp, win_ks, win_vs = [], [], [], []
    for layer in range(depth):
        i = layer // 2
        w_mix_in, w_mix_out, w_gate, w_up, w_down = w_bf
        if layer % 2 == 0:
            u, v, z, gb = _even_proj(xn, w_mix_in)
            wmix, bmix = _mix_tables(sgu_w[i], sgu_b[i], dec_batch, dec_seq)
            st = state_conv[i]
            zero = jnp.zeros_like(st[:, :1])
            s1 = jnp.concatenate([st[:, 1:2], zero, zero, zero], axis=1)
            s2 = jnp.concatenate([st[:, 0:1], st[:, 1:2], zero, zero], axis=1)
            state_rows = jnp.stack([s1.reshape(n_s, -1), s2.reshape(n_s, -1)])
            a = _even_mix(u, v, z, gb, wmix, bmix, conv_w[i], state_rows)
            y = _proj(a, w_mix_out, "even_out")
            zc = z.shape[1]
            conv_p.append(jnp.stack([z[(b + 1) * seq - 2:(b + 1) * seq] for b in range(batch)]))
            conv_s.append(z[n_p:].reshape(dec_batch, dec_seq, zc)[:, dec_seq - 2:])
            chunk_v_s.append(v[n_p:].reshape(dec_batch, dec_seq, A_HEADS, zc // A_HEADS))
        else:
            qkv = _proj(xn, w_mix_in, "qkv")
            qs = qkv[n_p:, :dq].reshape(dec_batch, dec_seq, N_KV_HEADS, GQA_GROUP, HEAD_DIM)
            qs = qs.transpose(0, 2, 3, 1, 4).reshape(dec_batch, N_KV_HEADS, GQA_GROUP * dec_seq, HEAD_DIM)
            k_new = qkv[n_p:, dq:dq + dkv].reshape(dec_batch, dec_seq, dkv)
            v_new = qkv[n_p:, dq + dkv:].reshape(dec_batch, dec_seq, dkv)
            pad = ((0, 0), (0, PAD_NEW_KEYS - dec_seq), (0, 0))
            sink_rows = jnp.repeat(attn_sinks[i].astype(F32), dec_seq).reshape(N_KV_HEADS, GQA_GROUP * dec_seq, 1)
            o_s = _attn_sample(qs, jnp.pad(k_new, pad), jnp.pad(v_new, pad),
                               cache_win_k[i].reshape(dec_batch, -1, dkv),
                               cache_win_v[i].reshape(dec_batch, -1, dkv),
                               bias_c, bias_n, sink_rows)
            o_s = o_s.reshape(dec_batch, N_KV_HEADS, GQA_GROUP, dec_seq, HEAD_DIM)
            o_s = o_s.transpose(0, 3, 1, 2, 4).reshape(n_s, dq).astype(BF16)
            sink_lanes = jnp.repeat(attn_sinks[i].astype(F32), CHUNK).reshape(-1, 1, ATTN_HEADS * CHUNK)
            y = _proj(_attn_prompt(qkv, o_s, bias_t, sink_lanes), w_mix_out, "attn_out")
            win = CHUNK
            tail = lambda c0: jnp.stack([qkv[(b + 1) * seq - win:(b + 1) * seq, c0:c0 + dkv]
                                         for b in range(batch)]).reshape(batch, win, N_KV_HEADS, HEAD_DIM)
            win_kp.append(tail(dq))
            win_vp.append(tail(dq + dkv))
            k_all = jnp.concatenate([cache_win_k[i], k_new.reshape(dec_batch, dec_seq, N_KV_HEADS, HEAD_DIM)], axis=1)
            v_all = jnp.concatenate([cache_win_v[i], v_new.reshape(dec_batch, dec_seq, N_KV_HEADS, HEAD_DIM)], axis=1)
            win_ks.append(k_all[:, k_all.shape[1] - win:])
            win_vs.append(v_all[:, v_all.shape[1] - win:])
        if layer == 0:
            x, xn = _resnorm_first(x_p, x_s, y, norm_mix_post[layer], norm_ffn_pre[layer])
        else:
            x, xn = _resnorm(x, y, norm_mix_post[layer], norm_ffn_pre[layer])
        y, w_bf = _ffn(xn, w_gate, w_up, w_down, layer_weights(layer + 1) if layer + 1 < depth else ())
        if layer + 1 < depth:
            x, xn = _resnorm(x, y, norm_ffn_post[layer], norm_mix_pre[layer + 1])
    out_p, out_s = _res_last(x, y, norm_ffn_post[depth - 1], n_p)

    return (out_p.reshape(batch, seq, d), out_s.reshape(dec_batch, dec_seq, d),
            jnp.stack(conv_p), jnp.stack(conv_s), jnp.stack(win_kp), jnp.stack(win_vp),
            jnp.stack(win_ks), jnp.stack(win_vs), jnp.stack(chunk_v_s))
```

```python
import functools
import math

import jax
import jax.numpy as jnp
from jax import lax
from jax.experimental import pallas as pl
from jax.experimental.pallas import tpu as pltpu

F32 = jnp.float32
BF16 = jnp.bfloat16

EPS = 1e-6
NEG_INF = -1e30
CHUNK = 128
A_HEADS = 8
N_KV_HEADS = 8
GQA_GROUP = 8
HEAD_DIM = 64
N_BUCKETS = 32
MAX_DISTANCE = 128
SEQ_CHUNKS = 16

VMEM_LIMIT_BYTES = 56 * 1024 * 1024

TM = 1040
TR = 320
TN = 512
TN_EVEN = 256
TF = 256
TC = 256
ATTN_HEADS = 8
PAD_NEW_KEYS = 16


def _params(*sem):
    return pltpu.CompilerParams(dimension_semantics=sem, vmem_limit_bytes=VMEM_LIMIT_BYTES)


def _rms(x, g):
    return x * lax.rsqrt(jnp.mean(x * x, axis=-1, keepdims=True) + EPS) * g


def _two_source_specs(d, n_blocks):
    return [pl.BlockSpec((CHUNK, d), lambda i: (jnp.minimum(i, n_blocks - 2), 0)),
            pl.BlockSpec((CHUNK, d), lambda i: (0, 0))]


def _on_source(body, xp_ref, xs_ref):
    i = pl.program_id(0)
    last = pl.num_programs(0) - 1

    @pl.when(i < last)
    def _():
        body(xp_ref[...])

    @pl.when(i == last)
    def _():
        body(xs_ref[...])


def _norm_first_kernel(xp_ref, xs_ref, g_ref, xn_ref):
    def body(x):
        xn_ref[...] = _rms(x, g_ref[...]).astype(BF16)

    _on_source(body, xp_ref, xs_ref)


def _norm_first(xp, xs, g):
    d = xp.shape[1]
    m = xp.shape[0] + xs.shape[0]
    nb = m // CHUNK
    row = pl.BlockSpec((CHUNK, d), lambda i: (i, 0))
    vec = pl.BlockSpec((1, d), lambda i: (0, 0))
    return pl.pallas_call(
        _norm_first_kernel, grid=(nb,), in_specs=_two_source_specs(d, nb) + [vec], out_specs=row,
        out_shape=jax.ShapeDtypeStruct((m, d), BF16),
        compiler_params=_params("arbitrary"), name="norm_first",
    )(xp, xs, g.reshape(1, d))


def _resnorm_first_kernel(xp_ref, xs_ref, y_ref, gp_ref, gn_ref, xo_ref, xn_ref):
    def body(x):
        xo = x + _rms(y_ref[...], gp_ref[...])
        xo_ref[...] = xo
        xn_ref[...] = _rms(xo, gn_ref[...]).astype(BF16)

    _on_source(body, xp_ref, xs_ref)


def _resnorm_first(xp, xs, y, g_post, g_next):
    m, d = y.shape
    nb = m // CHUNK
    row = pl.BlockSpec((CHUNK, d), lambda i: (i, 0))
    vec = pl.BlockSpec((1, d), lambda i: (0, 0))
    return pl.pallas_call(
        _resnorm_first_kernel, grid=(nb,), in_specs=_two_source_specs(d, nb) + [row, vec, vec],
        out_specs=[row, row],
        out_shape=[jax.ShapeDtypeStruct((m, d), F32), jax.ShapeDtypeStruct((m, d), BF16)],
        compiler_params=_params("arbitrary"), name="resnorm_first",
    )(xp, xs, y, g_post.reshape(1, d), g_next.reshape(1, d))


def _resnorm_kernel(x_ref, y_ref, gp_ref, gn_ref, xo_ref, xn_ref):
    xo = x_ref[...] + _rms(y_ref[...], gp_ref[...])
    xo_ref[...] = xo
    xn_ref[...] = _rms(xo, gn_ref[...]).astype(BF16)


def _resnorm(x, y, g_post, g_next):
    m, d = x.shape
    row = pl.BlockSpec((TR, d), lambda i: (i, 0))
    vec = pl.BlockSpec((1, d), lambda i: (0, 0))
    return pl.pallas_call(
        _resnorm_kernel, grid=(m // TR,), in_specs=[row, row, vec, vec],
        out_specs=[row, row],
        out_shape=[jax.ShapeDtypeStruct((m, d), F32), jax.ShapeDtypeStruct((m, d), BF16)],
        compiler_params=_params("parallel"), name="resnorm",
    )(x, y, g_post.reshape(1, d), g_next.reshape(1, d))


def _res_last_kernel(x_ref, y_ref, gp_ref, op_ref, os_ref):
    i = pl.program_id(0)
    last = pl.num_programs(0) - 1
    xo = x_ref[...] + _rms(y_ref[...], gp_ref[...])

    @pl.when(i < last)
    def _():
        op_ref[...] = xo

    @pl.when(i == last)
    def _():
        os_ref[...] = xo


def _res_last(x, y, g_post, n_prompt):
    m, d = x.shape
    nb = m // CHUNK
    row = pl.BlockSpec((CHUNK, d), lambda i: (i, 0))
    vec = pl.BlockSpec((1, d), lambda i: (0, 0))
    return pl.pallas_call(
        _res_last_kernel, grid=(nb,), in_specs=[row, row, vec],
        out_specs=[pl.BlockSpec((CHUNK, d), lambda i: (jnp.minimum(i, nb - 2), 0)),
                   pl.BlockSpec((CHUNK, d), lambda i: (0, 0))],
        out_shape=[jax.ShapeDtypeStruct((n_prompt, d), F32),
                   jax.ShapeDtypeStruct((m - n_prompt, d), F32)],
        compiler_params=_params("arbitrary"), name="res_last",
    )(x, y, g_post.reshape(1, d))


def _cast_kernel(w_ref, o_ref):
    o_ref[...] = w_ref[...].astype(BF16)


def _cast(w, layer):
    _, r, c = w.shape
    return pl.pallas_call(
        _cast_kernel,
        grid=(r // TC,),
        in_specs=[pl.BlockSpec((None, TC, c), lambda i: (layer, i, 0))],
        out_specs=pl.BlockSpec((TC, c), lambda i: (i, 0)),
        out_shape=jax.ShapeDtypeStruct((r, c), BF16),
        compiler_params=_params("parallel"),
        name="cast",
    )(w)


def _side_casts(src_refs, dst_refs):
    for src, dst in zip(src_refs, dst_refs):
        dst[...] = src[...].astype(BF16)


def _side_cast_specs(w, layer, n_i, n_f, by_rows=False):
    _, r, c = w.shape
    shape = jax.ShapeDtypeStruct((r, c), BF16)
    if by_rows:
        n_r = max(n for n in range(1, n_i * n_f + 1) if r % n == 0 and (r // n) % 16 == 0)
        step = lambda i, f: jnp.minimum(i * n_f + f, n_r - 1)
        return (pl.BlockSpec((None, r // n_r, c), lambda i, f: (layer, step(i, f), 0)),
                pl.BlockSpec((r // n_r, c), lambda i, f: (step(i, f), 0)), shape)
    br = r // n_i
    assert br * n_i == r and br % 16 == 0
    n_c = max(n for n in range(1, n_f + 1) if c % n == 0 and (c // n) % 128 == 0)
    bc = c // n_c
    col = lambda f: jnp.minimum(f, n_c - 1)
    return (pl.BlockSpec((None, br, bc), lambda i, f: (layer, i, col(f))),
            pl.BlockSpec((br, bc), lambda i, f: (i, col(f))), shape)


def _proj_kernel(n_side, a_ref, w_ref, *refs):
    side_src, o_ref, side_dst = refs[:n_side], refs[n_side], refs[n_side + 1:]
    o_ref[...] = jnp.dot(a_ref[...], w_ref[...], preferred_element_type=F32)
    _side_casts(side_src, side_dst)


def _proj(a, w, name, side=()):
    m, k = a.shape
    n = w.shape[1]
    n_i, n_j = m // TM, n // TN
    specs = [_side_cast_specs(sw, layer, n_i, n_j, by_rows=True) for sw, layer in side]
    outs = pl.pallas_call(
        functools.partial(_proj_kernel, len(side)),
        grid=(n_i, n_j),
        in_specs=[pl.BlockSpec((TM, k), lambda i, j: (i, 0)),
                  pl.BlockSpec((k, TN), lambda i, j: (0, j))] + [sp[0] for sp in specs],
        out_specs=[pl.BlockSpec((TM, TN), lambda i, j: (i, j))] + [sp[1] for sp in specs],
        out_shape=[jax.ShapeDtypeStruct((m, n), F32)] + [sp[2] for sp in specs],
        compiler_params=_params("arbitrary", "arbitrary"),
        name=name,
    )(a, w, *[sw for sw, _ in side])
    return outs[0], outs[1:]


def _even_proj_kernel(xn_ref, wu_ref, wv_ref, wx_ref, wc_ref, wb_ref,
                      u_ref, v_ref, z_ref, gb_ref):
    x = xn_ref[...]

    def dot(w_ref):
        return jnp.dot(x, w_ref[...], preferred_element_type=F32)

    u_ref[...] = jax.nn.gelu(dot(wu_ref), approximate=True)
    v_ref[...] = jax.nn.gelu(dot(wv_ref), approximate=True)
    z_ref[...] = dot(wc_ref) * dot(wx_ref)
    gb_ref[...] = dot(wb_ref)


def _even_proj(xn, w_in):
    m, d = xn.shape
    width = w_in.shape[1] // 5
    nt = width // TN_EVEN

    def wspec(group):
        return pl.BlockSpec((d, TN_EVEN), lambda i, j: (0, group * nt + j))

    out = pl.BlockSpec((TM, TN_EVEN), lambda i, j: (i, j))
    return pl.pallas_call(
        _even_proj_kernel,
        grid=(m // TM, nt),
        in_specs=[pl.BlockSpec((TM, d), lambda i, j: (i, 0)),
                  wspec(0), wspec(1), wspec(2), wspec(3), wspec(4)],
        out_specs=[out] * 4,
        out_shape=[jax.ShapeDtypeStruct((m, width), F32)] * 4,
        compiler_params=_params("parallel", "arbitrary"),
        name="even_proj",
    )(xn, w_in, w_in, w_in, w_in, w_in)


def _even_mix_kernel(u_ref, v_ref, z_ref, gb_ref, zh_ref, w_ref, b_ref, cw_ref, s_ref, ab_ref):
    c = pl.program_id(0)
    n_prompt = pl.num_programs(0) - 1
    hd = u_ref.shape[1] // A_HEADS
    half = u_ref.shape[1]

    v = v_ref[...].astype(BF16)
    for h in range(A_HEADS):
        cols = slice(h * hd, (h + 1) * hd)
        mixed = jnp.dot(w_ref[0, h], v[:, cols], preferred_element_type=F32)
        mixed = mixed + b_ref[0][:, h:h + 1]
        ab_ref[:, cols] = (u_ref[:, cols] * mixed).astype(BF16)

    z = z_ref[...]
    row = lax.broadcasted_iota(jnp.int32, (CHUNK, 1), 0)
    r1 = pltpu.roll(z, 1, 0)
    r2 = pltpu.roll(z, 2, 0)
    cw = cw_ref[...]

    def emit(zp1, zp2):
        conv = cw[0:1] * zp2 + cw[1:2] * zp1 + cw[2:3] * z
        ab_ref[:, half:] = (gb_ref[...] * conv).astype(BF16)

    @pl.when(c < n_prompt)
    def _():
        keep = c % SEQ_CHUNKS != 0
        h1 = jnp.where(keep, zh_ref[7:8, :], 0.0)
        h2 = jnp.where(keep, zh_ref[6:7, :], 0.0)
        emit(jnp.where(row >= 1, r1, h1),
             jnp.where(row >= 2, r2, jnp.where(row == 1, h1, h2)))

    @pl.when(c == n_prompt)
    def _():
        t = row % 4
        emit(jnp.where(t >= 1, r1, s_ref[0]), jnp.where(t >= 2, r2, s_ref[1]))


def _even_mix(u, v, z, gb, wmix, bmix, conv_w, state_rows):
    m, half = u.shape
    nchunks = m // CHUNK
    blk = pl.BlockSpec((CHUNK, half), lambda c: (c, 0))
    sel = lambda c: (c // (nchunks - 1), 0, 0, 0)
    return pl.pallas_call(
        _even_mix_kernel,
        grid=(nchunks,),
        in_specs=[blk, blk, blk, blk,
                  pl.BlockSpec((8, half), lambda c: (jnp.maximum(c * (CHUNK // 8) - 1, 0), 0)),
                  pl.BlockSpec((1, A_HEADS, CHUNK, CHUNK), sel),
                  pl.BlockSpec((1, CHUNK, A_HEADS), lambda c: (c // (nchunks - 1), 0, 0)),
                  pl.BlockSpec((3, half), lambda c: (0, 0)),
                  pl.BlockSpec((2, CHUNK, half), lambda c: (0, 0, 0))],
        out_specs=pl.BlockSpec((CHUNK, 2 * half), lambda c: (c, 0)),
        out_shape=jax.ShapeDtypeStruct((m, 2 * half), BF16),
        compiler_params=_params("arbitrary"),
        name="even_mix",
    )(u, v, z, gb, z, wmix, bmix, conv_w, state_rows)


def _attn_prompt_kernel(q_ref, kc_ref, vc_ref, kp_ref, vp_ref, bias_ref, sink_ref, os_ref, o_ref):
    i = pl.program_id(0)
    last = pl.num_programs(0) - 1

    @pl.when(i < last)
    def _():
        _attn_prompt_block(i, q_ref, kc_ref, vc_ref, kp_ref, vp_ref, bias_ref, sink_ref, o_ref)

    @pl.when(i == last)
    def _():
        o_ref[...] = os_ref[...]


def _attn_prompt_block(i, q_ref, kc_ref, vc_ref, kp_ref, vp_ref, bias_ref, sink_ref, o_ref):
    nh = ATTN_HEADS
    width, lanes = nh * HEAD_DIM, nh * CHUNK
    kk = jnp.concatenate([kp_ref[...], kc_ref[...]], axis=0).astype(BF16)
    vt = jnp.concatenate([vp_ref[...], vc_ref[...]], axis=0).T.astype(BF16)
    key = lax.broadcasted_iota(jnp.int32, (2 * CHUNK, lanes), 0)
    qry = lax.broadcasted_iota(jnp.int32, (2 * CHUNK, lanes), 1) % CHUNK
    dist = qry + CHUNK - key
    has_prev = i % SEQ_CHUNKS != 0
    valid = (dist >= 0) & (dist < CHUNK) & (has_prev | (key >= CHUNK))
    for b in range(N_KV_HEADS * GQA_GROUP // nh):
        g = (b * nh) // GQA_GROUP
        gc = slice(g * HEAD_DIM, (g + 1) * HEAD_DIM)
        qt = (q_ref[:, b * width:(b + 1) * width].T * (HEAD_DIM ** -0.5)).astype(BF16)
        qt = jnp.concatenate([qt[h * HEAD_DIM:(h + 1) * HEAD_DIM] for h in range(nh)], axis=1)
        s = jnp.dot(kk[:, gc], qt, preferred_element_type=F32) + bias_ref[b]
        s = jnp.where(valid, s, NEG_INF)
        sk = sink_ref[b]
        m = jnp.maximum(jnp.max(s, axis=0, keepdims=True), sk)
        e = jnp.exp(s - m)
        den = jnp.sum(e, axis=0, keepdims=True) + jnp.exp(sk - m)
        ot = jnp.dot(vt[gc, :], e.astype(BF16), preferred_element_type=F32) / den
        for pair in range(nh // 2):
            lo = 2 * pair * CHUNK
            o = jnp.concatenate([ot[:, lo:lo + CHUNK], ot[:, lo + CHUNK:lo + 2 * CHUNK]], axis=0).T
            c0 = b * width + pair * 2 * HEAD_DIM
            o_ref[:, c0:c0 + 2 * HEAD_DIM] = o.astype(BF16)


def _attn_prompt(qkv, o_sample, bias_t, sink_rows):
    m = qkv.shape[0]
    nb = m // CHUNK
    dq = N_KV_HEADS * GQA_GROUP * HEAD_DIM
    dkv = N_KV_HEADS * HEAD_DIM
    kcol, vcol = dq // dkv, dq // dkv + 1
    cur = lambda i: jnp.minimum(i, nb - 2)
    prev = lambda i: jnp.maximum(cur(i) - 1, 0)
    return pl.pallas_call(
        _attn_prompt_kernel,
        grid=(nb,),
        in_specs=[pl.BlockSpec((CHUNK, dq), lambda i: (cur(i), 0)),
                  pl.BlockSpec((CHUNK, dkv), lambda i: (cur(i), kcol)),
                  pl.BlockSpec((CHUNK, dkv), lambda i: (cur(i), vcol)),
                  pl.BlockSpec((CHUNK, dkv), lambda i: (prev(i), kcol)),
                  pl.BlockSpec((CHUNK, dkv), lambda i: (prev(i), vcol)),
                  pl.BlockSpec(bias_t.shape, lambda i: (0, 0, 0)),
                  pl.BlockSpec(sink_rows.shape, lambda i: (0, 0, 0)),
                  pl.BlockSpec(o_sample.shape, lambda i: (0, 0))],
        out_specs=pl.BlockSpec((CHUNK, dq), lambda i: (i, 0)),
        out_shape=jax.ShapeDtypeStruct((m, dq), BF16),
        compiler_params=_params("arbitrary"),
        name="attn_prompt",
    )(qkv, qkv, qkv, qkv, qkv, bias_t, sink_rows, o_sample)


def _attn_sample_kernel(q_ref, kn_ref, vn_ref, kc_ref, vc_ref, bc_ref, bn_ref, sink_ref, o_ref):
    kc = kc_ref[0].astype(BF16)
    vc = vc_ref[0].astype(BF16)
    kn = kn_ref[0].astype(BF16)
    vn = vn_ref[0].astype(BF16)
    rows = q_ref.shape[2]
    dec = rows // GQA_GROUP
    t_c = lax.broadcasted_iota(jnp.int32, (rows, CHUNK), 0) % dec
    j_c = lax.broadcasted_iota(jnp.int32, (rows, CHUNK), 1)
    valid_c = j_c > t_c
    t_n = lax.broadcasted_iota(jnp.int32, (rows, PAD_NEW_KEYS), 0) % dec
    j_n = lax.broadcasted_iota(jnp.int32, (rows, PAD_NEW_KEYS), 1)
    valid_n = j_n <= t_n
    nt = (((1,), (1,)), ((), ()))
    for g in range(N_KV_HEADS):
        gc = slice(g * HEAD_DIM, (g + 1) * HEAD_DIM)
        q = (q_ref[0, g] * (HEAD_DIM ** -0.5)).astype(BF16)
        sc = lax.dot_general(q, kc[:, gc], nt, preferred_element_type=F32) + bc_ref[g]
        sn = lax.dot_general(q, kn[:, gc], nt, preferred_element_type=F32) + bn_ref[g]
        sc = jnp.where(valid_c, sc, NEG_INF)
        sn = jnp.where(valid_n, sn, NEG_INF)
        sk = sink_ref[g]
        m = jnp.maximum(jnp.maximum(jnp.max(sc, axis=-1, keepdims=True),
                                    jnp.max(sn, axis=-1, keepdims=True)), sk)
        pc = jnp.exp(sc - m)
        pn = jnp.exp(sn - m)
        den = (jnp.sum(pc, axis=-1, keepdims=True) + jnp.sum(pn, axis=-1, keepdims=True)
               + jnp.exp(sk - m))
        o = (jnp.dot(pc.astype(BF16), vc[:, gc], preferred_element_type=F32)
             + jnp.dot(pn.astype(BF16), vn[:, gc], preferred_element_type=F32))
        o_ref[0, g] = o / den


def _attn_sample(q, kn, vn, kc, vc, bias_c, bias_n, sink_rows):
    nb, _, rows, _ = q.shape
    dkv = N_KV_HEADS * HEAD_DIM
    full = lambda a: pl.BlockSpec(a.shape, lambda b: (0,) * a.ndim)
    return pl.pallas_call(
        _attn_sample_kernel,
        grid=(nb,),
        in_specs=[pl.BlockSpec((1, N_KV_HEADS, rows, HEAD_DIM), lambda b: (b, 0, 0, 0)),
                  pl.BlockSpec((1, PAD_NEW_KEYS, dkv), lambda b: (b, 0, 0)),
                  pl.BlockSpec((1, PAD_NEW_KEYS, dkv), lambda b: (b, 0, 0)),
                  pl.BlockSpec((1, CHUNK, dkv), lambda b: (b, 0, 0)),
                  pl.BlockSpec((1, CHUNK, dkv), lambda b: (b, 0, 0)),
                  full(bias_c), full(bias_n), full(sink_rows)],
        out_specs=pl.BlockSpec((1, N_KV_HEADS, rows, HEAD_DIM), lambda b: (b, 0, 0, 0)),
        out_shape=jax.ShapeDtypeStruct(q.shape, F32),
        compiler_params=_params("parallel"),
        name="attn_sample",
    )(q, kn, vn, kc, vc, bias_c, bias_n, sink_rows)


def _ffn_kernel(n_side, xn_ref, wg_ref, wu_ref, wd_ref, *refs):
    side_src, y_ref, side_dst = refs[:n_side], refs[n_side], refs[n_side + 1:]
    f = pl.program_id(1)
    x = xn_ref[...]
    gate = jnp.dot(x, wg_ref[...], preferred_element_type=F32)
    up = jnp.dot(x, wu_ref[...], preferred_element_type=F32)
    h = (jax.nn.silu(gate) * up).astype(BF16)

    @pl.when(f == 0)
    def _():
        y_ref[...] = jnp.dot(h, wd_ref[...], preferred_element_type=F32)
        _side_casts(side_src, side_dst)

    @pl.when(f > 0)
    def _():
        y_ref[...] += jnp.dot(h, wd_ref[...], preferred_element_type=F32)
        _side_casts(side_src, side_dst)


def _ffn(xn, wg, wu, wd, side=()):
    m, d = xn.shape
    dff = wg.shape[1]
    n_i, n_f = m // TM, dff // TF
    specs = [_side_cast_specs(w, layer, n_i, n_f) for w, layer in side]
    outs = pl.pallas_call(
        functools.partial(_ffn_kernel, len(side)),
        grid=(n_i, n_f),
        in_specs=[pl.BlockSpec((TM, d), lambda i, f: (i, 0)),
                  pl.BlockSpec((d, TF), lambda i, f: (0, f)),
                  pl.BlockSpec((d, TF), lambda i, f: (0, f)),
                  pl.BlockSpec((TF, d), lambda i, f: (f, 0))] + [s[0] for s in specs],
        out_specs=[pl.BlockSpec((TM, d), lambda i, f: (i, 0), pipeline_mode=pl.Buffered(1))]
        + [s[1] for s in specs],
        out_shape=[jax.ShapeDtypeStruct((m, d), F32)] + [s[2] for s in specs],
        compiler_params=_params("arbitrary", "arbitrary"),
        name="ffn",
    )(xn, wg, wu, wd, *[w for w, _ in side])
    return outs[0], outs[1:]


def _t5_bucket(dist):
    max_exact = N_BUCKETS // 2
    d = jnp.maximum(dist, max_exact).astype(F32)
    large = max_exact + (jnp.log(d / max_exact) / math.log(MAX_DISTANCE / max_exact)
                         * (N_BUCKETS - max_exact)).astype(jnp.int32)
    return jnp.where(dist < max_exact, dist, jnp.minimum(large, N_BUCKETS - 1))


def _bias_tables(rel_bias, dec_seq):
    n_heads = rel_bias.shape[1]
    assert n_heads == N_KV_HEADS * GQA_GROUP
    by_dist = rel_bias.astype(F32)[_t5_bucket(jnp.arange(CHUNK))].T
    span = 3 * CHUNK
    row = by_dist[:, jnp.clip(jnp.arange(span) - (CHUNK - 1), 0, CHUNK - 1)]
    flat = jnp.broadcast_to(row[:, None, :], (n_heads, 2 * CHUNK, span)).reshape(n_heads, -1)
    start = 2 * CHUNK - 1
    bias_t = flat[:, start:start + 2 * CHUNK * (span - 1)].reshape(n_heads, 2 * CHUNK, span - 1)
    bias_t = bias_t[:, :, :CHUNK]
    nb = n_heads // ATTN_HEADS
    bias_t = bias_t.reshape(nb, ATTN_HEADS, 2 * CHUNK, CHUNK).transpose(0, 2, 1, 3)
    bias_t = bias_t.reshape(nb, 2 * CHUNK, ATTN_HEADS * CHUNK)
    t = jnp.arange(dec_seq)[:, None]
    dist_c = jnp.clip(t + CHUNK - jnp.arange(CHUNK)[None, :], 0, CHUNK - 1)
    dist_n = jnp.clip(t - jnp.arange(PAD_NEW_KEYS)[None, :], 0, CHUNK - 1)
    rows = GQA_GROUP * dec_seq
    bias_c = by_dist[:, dist_c].reshape(N_KV_HEADS, rows, CHUNK)
    bias_n = by_dist[:, dist_n].reshape(N_KV_HEADS, rows, PAD_NEW_KEYS)
    return bias_t, bias_c, bias_n


def _mix_tables(w_s, b_s, dec_batch, dec_seq):
    assert dec_batch * dec_seq == CHUNK
    w_p = jnp.tril(w_s)
    small = jnp.tril(w_s[:, :dec_seq, :dec_seq])
    eye = jnp.eye(dec_batch, dtype=w_s.dtype)
    w_d = jnp.einsum("bc,hij->hbicj", eye, small).reshape(w_s.shape)
    b_p = b_s.T
    b_d = jnp.tile(b_s[:, :dec_seq].T, (dec_batch, 1))
    return jnp.stack([w_p, w_d]).astype(BF16), jnp.stack([b_p, b_d])


def kernel(x_prompt, x_sample, state_conv, cache_win_k, cache_win_v, norm_mix_pre, norm_mix_post, norm_ffn_pre, norm_ffn_post, w_in_even, w_out_even, sgu_w, sgu_b, conv_w, w_qkv_odd, w_o_odd, attn_sinks, rel_bias, ffn_w_gate, ffn_w_up, ffn_w_down):
    batch, seq, d = x_prompt.shape
    dec_batch, dec_seq, _ = x_sample.shape
    depth = norm_mix_pre.shape[0]
    n_p = batch * seq
    n_s = dec_batch * dec_seq
    dq = N_KV_HEADS * GQA_GROUP * HEAD_DIM
    dkv = N_KV_HEADS * HEAD_DIM
    assert seq == SEQ_CHUNKS * CHUNK and n_s == CHUNK and dec_seq <= PAD_NEW_KEYS

    x_p, x_s = x_prompt.reshape(n_p, d), x_sample.reshape(n_s, d)
    xn = _norm_first(x_p, x_s, norm_mix_pre[0])
    bias_t, bias_c, bias_n = _bias_tables(rel_bias, dec_seq)

    def layer_weights(layer):
        mixer = ([(w_in_even, layer // 2), (w_out_even, layer // 2)] if layer % 2 == 0
                 else [(w_qkv_odd, layer // 2), (w_o_odd, layer // 2)])
        return mixer + [(ffn_w_gate, layer), (ffn_w_up, layer), (ffn_w_down, layer)]

    w_bf = [_cast(w_in_even, 0), _cast(w_out_even, 0)]
    w_down0 = _cast(ffn_w_down, 0)

    conv_p, conv_s, chunk_v_s = [], [], []
    win_kp, win_vp, win_ks, win_vs = [], [], [], []
    for layer in range(depth):
        i = layer // 2
        w_mix_in, w_mix_out = w_bf[:2]
        if layer % 2 == 0:
            u, v, z, gb = _even_proj(xn, w_mix_in)
            wmix, bmix = _mix_tables(sgu_w[i], sgu_b[i], dec_batch, dec_seq)
            st = state_conv[i]
            zero = jnp.zeros_like(st[:, :1])
            s1 = jnp.concatenate([st[:, 1:2], zero, zero, zero], axis=1)
            s2 = jnp.concatenate([st[:, 0:1], st[:, 1:2], zero, zero], axis=1)
            state_rows = jnp.stack([s1.reshape(n_s, -1), s2.reshape(n_s, -1)])
            a = _even_mix(u, v, z, gb, wmix, bmix, conv_w[i], state_rows)
            y, cast_gu = _proj(a, w_mix_out, "even_out",
                               [(ffn_w_gate, 0), (ffn_w_up, 0)] if layer == 0 else ())
            if layer == 0:
                w_bf = w_bf[:2] + list(cast_gu) + [w_down0]
            zc = z.shape[1]
            conv_p.append(jnp.stack([z[(b + 1) * seq - 2:(b + 1) * seq] for b in range(batch)]))
            conv_s.append(z[n_p:].reshape(dec_batch, dec_seq, zc)[:, dec_seq - 2:])
            chunk_v_s.append(v[n_p:].reshape(dec_batch, dec_seq, A_HEADS, zc // A_HEADS))
        else:
            qkv, _ = _proj(xn, w_mix_in, "qkv")
            qs = qkv[n_p:, :dq].reshape(dec_batch, dec_seq, N_KV_HEADS, GQA_GROUP, HEAD_DIM)
            qs = qs.transpose(0, 2, 3, 1, 4).reshape(dec_batch, N_KV_HEADS, GQA_GROUP * dec_seq, HEAD_DIM)
            k_new = qkv[n_p:, dq:dq + dkv].reshape(dec_batch, dec_seq, dkv)
            v_new = qkv[n_p:, dq + dkv:].reshape(dec_batch, dec_seq, dkv)
            pad = ((0, 0), (0, PAD_NEW_KEYS - dec_seq), (0, 0))
            sink_rows = jnp.repeat(attn_sinks[i].astype(F32), dec_seq).reshape(N_KV_HEADS, GQA_GROUP * dec_seq, 1)
            o_s = _attn_sample(qs, jnp.pad(k_new, pad), jnp.pad(v_new, pad),
                               cache_win_k[i].reshape(dec_batch, -1, dkv),
                               cache_win_v[i].reshape(dec_batch, -1, dkv),
                               bias_c, bias_n, sink_rows)
            o_s = o_s.reshape(dec_batch, N_KV_HEADS, GQA_GROUP, dec_seq, HEAD_DIM)
            o_s = o_s.transpose(0, 3, 1, 2, 4).reshape(n_s, dq).astype(BF16)
            sink_lanes = jnp.repeat(attn_sinks[i].astype(F32), CHUNK).reshape(-1, 1, ATTN_HEADS * CHUNK)
            y, _ = _proj(_attn_prompt(qkv, o_s, bias_t, sink_lanes), w_mix_out, "attn_out")
            win = CHUNK
            tail = lambda c0: jnp.stack([qkv[(b + 1) * seq - win:(b + 1) * seq, c0:c0 + dkv]
                                         for b in range(batch)]).reshape(batch, win, N_KV_HEADS, HEAD_DIM)
            win_kp.append(tail(dq))
            win_vp.append(tail(dq + dkv))
            win_ks.append(k_new.reshape(dec_batch, dec_seq, N_KV_HEADS, HEAD_DIM))
            win_vs.append(v_new.reshape(dec_batch, dec_seq, N_KV_HEADS, HEAD_DIM))
        if layer == 0:
            x, xn = _resnorm_first(x_p, x_s, y, norm_mix_post[layer], norm_ffn_pre[layer])
        else:
            x, xn = _resnorm(x, y, norm_mix_post[layer], norm_ffn_pre[layer])
        w_gate, w_up, w_down = w_bf[2:]
        y, w_bf = _ffn(xn, w_gate, w_up, w_down, layer_weights(layer + 1) if layer + 1 < depth else ())
        if layer + 1 < depth:
            x, xn = _resnorm(x, y, norm_ffn_post[layer], norm_mix_pre[layer + 1])
    out_p, out_s = _res_last(x, y, norm_ffn_post[depth - 1], n_p)

    def new_window(cache, new_rows):
        keep = cache.shape[2] + dec_seq - CHUNK
        return jnp.concatenate([cache[:, :, keep:], jnp.stack(new_rows)], axis=2)

    return (out_p.reshape(batch, seq, d), out_s.reshape(dec_batch, dec_seq, d),
            jnp.stack(conv_p), jnp.stack(conv_s), jnp.stack(win_kp), jnp.stack(win_vp),
            new_window(cache_win_k, win_ks), new_window(cache_win_v, win_vs), jnp.stack(chunk_v_s))
```

```python
import functools
import math

import jax
import jax.numpy as jnp
from jax import lax
from jax.experimental import pallas as pl
from jax.experimental.pallas import tpu as pltpu

F32 = jnp.float32
BF16 = jnp.bfloat16

EPS = 1e-6
NEG_INF = -1e30
CHUNK = 128
A_HEADS = 8
N_KV_HEADS = 8
GQA_GROUP = 8
HEAD_DIM = 64
N_BUCKETS = 32
MAX_DISTANCE = 128
SEQ_CHUNKS = 16

VMEM_LIMIT_BYTES = 56 * 1024 * 1024

TM = 1040
TR = 320
TN = 1024
TN_EVEN = 256
TF = 256
TC = 256
ATTN_HEADS = 8
PAD_NEW_KEYS = 16


def _params(*sem):
    return pltpu.CompilerParams(dimension_semantics=sem, vmem_limit_bytes=VMEM_LIMIT_BYTES)


def _rms(x, g):
    return x * lax.rsqrt(jnp.mean(x * x, axis=-1, keepdims=True) + EPS) * g


def _two_source_specs(d, n_blocks):
    return [pl.BlockSpec((CHUNK, d), lambda i: (jnp.minimum(i, n_blocks - 2), 0)),
            pl.BlockSpec((CHUNK, d), lambda i: (0, 0))]


def _on_source(body, xp_ref, xs_ref):
    i = pl.program_id(0)
    last = pl.num_programs(0) - 1

    @pl.when(i < last)
    def _():
        body(xp_ref[...])

    @pl.when(i == last)
    def _():
        body(xs_ref[...])


def _norm_first_kernel(xp_ref, xs_ref, g_ref, xn_ref):
    def body(x):
        xn_ref[...] = _rms(x, g_ref[...]).astype(BF16)

    _on_source(body, xp_ref, xs_ref)


def _norm_first(xp, xs, g):
    d = xp.shape[1]
    m = xp.shape[0] + xs.shape[0]
    nb = m // CHUNK
    row = pl.BlockSpec((CHUNK, d), lambda i: (i, 0))
    vec = pl.BlockSpec((1, d), lambda i: (0, 0))
    return pl.pallas_call(
        _norm_first_kernel, grid=(nb,), in_specs=_two_source_specs(d, nb) + [vec], out_specs=row,
        out_shape=jax.ShapeDtypeStruct((m, d), BF16),
        compiler_params=_params("arbitrary"), name="norm_first",
    )(xp, xs, g.reshape(1, d))


def _resnorm_first_kernel(xp_ref, xs_ref, y_ref, gp_ref, gn_ref, xo_ref, xn_ref):
    def body(x):
        xo = x + _rms(y_ref[...], gp_ref[...])
        xo_ref[...] = xo
        xn_ref[...] = _rms(xo, gn_ref[...]).astype(BF16)

    _on_source(body, xp_ref, xs_ref)


def _resnorm_first(xp, xs, y, g_post, g_next):
    m, d = y.shape
    nb = m // CHUNK
    row = pl.BlockSpec((CHUNK, d), lambda i: (i, 0))
    vec = pl.BlockSpec((1, d), lambda i: (0, 0))
    return pl.pallas_call(
        _resnorm_first_kernel, grid=(nb,), in_specs=_two_source_specs(d, nb) + [row, vec, vec],
        out_specs=[row, row],
        out_shape=[jax.ShapeDtypeStruct((m, d), F32), jax.ShapeDtypeStruct((m, d), BF16)],
        compiler_params=_params("arbitrary"), name="resnorm_first",
    )(xp, xs, y, g_post.reshape(1, d), g_next.reshape(1, d))


def _resnorm_kernel(x_ref, y_ref, gp_ref, gn_ref, xo_ref, xn_ref):
    xo = x_ref[...] + _rms(y_ref[...], gp_ref[...])
    xo_ref[...] = xo
    xn_ref[...] = _rms(xo, gn_ref[...]).astype(BF16)


def _resnorm(x, y, g_post, g_next):
    m, d = x.shape
    row = pl.BlockSpec((TR, d), lambda i: (i, 0))
    vec = pl.BlockSpec((1, d), lambda i: (0, 0))
    return pl.pallas_call(
        _resnorm_kernel, grid=(m // TR,), in_specs=[row, row, vec, vec],
        out_specs=[row, row],
        out_shape=[jax.ShapeDtypeStruct((m, d), F32), jax.ShapeDtypeStruct((m, d), BF16)],
        compiler_params=_params("parallel"), name="resnorm",
    )(x, y, g_post.reshape(1, d), g_next.reshape(1, d))


def _res_last_kernel(x_ref, y_ref, gp_ref, op_ref, os_ref):
    i = pl.program_id(0)
    last = pl.num_programs(0) - 1
    xo = x_ref[...] + _rms(y_ref[...], gp_ref[...])

    @pl.when(i < last)
    def _():
        op_ref[...] = xo

    @pl.when(i == last)
    def _():
        os_ref[...] = xo


def _res_last(x, y, g_post, n_prompt):
    m, d = x.shape
    nb = m // CHUNK
    row = pl.BlockSpec((CHUNK, d), lambda i: (i, 0))
    vec = pl.BlockSpec((1, d), lambda i: (0, 0))
    return pl.pallas_call(
        _res_last_kernel, grid=(nb,), in_specs=[row, row, vec],
        out_specs=[pl.BlockSpec((CHUNK, d), lambda i: (jnp.minimum(i, nb - 2), 0)),
                   pl.BlockSpec((CHUNK, d), lambda i: (0, 0))],
        out_shape=[jax.ShapeDtypeStruct((n_prompt, d), F32),
                   jax.ShapeDtypeStruct((m - n_prompt, d), F32)],
        compiler_params=_params("arbitrary"), name="res_last",
    )(x, y, g_post.reshape(1, d))


def _cast_kernel(w_ref, o_ref):
    o_ref[...] = w_ref[...].astype(BF16)


def _cast(w, layer):
    _, r, c = w.shape
    return pl.pallas_call(
        _cast_kernel,
        grid=(r // TC,),
        in_specs=[pl.BlockSpec((None, TC, c), lambda i: (layer, i, 0))],
        out_specs=pl.BlockSpec((TC, c), lambda i: (i, 0)),
        out_shape=jax.ShapeDtypeStruct((r, c), BF16),
        compiler_params=_params("parallel"),
        name="cast",
    )(w)


def _side_casts(src_refs, dst_refs):
    for src, dst in zip(src_refs, dst_refs):
        dst[...] = src[...].astype(BF16)


def _side_cast_specs(w, layer, n_i, n_f, by_rows=False):
    _, r, c = w.shape
    shape = jax.ShapeDtypeStruct((r, c), BF16)
    if by_rows:
        n_r = max(n for n in range(1, n_i * n_f + 1) if r % n == 0 and (r // n) % 16 == 0)
        step = lambda i, f: jnp.minimum(i * n_f + f, n_r - 1)
        return (pl.BlockSpec((None, r // n_r, c), lambda i, f: (layer, step(i, f), 0)),
                pl.BlockSpec((r // n_r, c), lambda i, f: (step(i, f), 0)), shape)
    br = r // n_i
    assert br * n_i == r and br % 16 == 0
    n_c = max(n for n in range(1, n_f + 1) if c % n == 0 and (c // n) % 128 == 0)
    bc = c // n_c
    col = lambda f: jnp.minimum(f, n_c - 1)
    return (pl.BlockSpec((None, br, bc), lambda i, f: (layer, i, col(f))),
            pl.BlockSpec((br, bc), lambda i, f: (i, col(f))), shape)


def _proj_kernel(n_side, a_ref, w_ref, *refs):
    side_src, o_ref, side_dst = refs[:n_side], refs[n_side], refs[n_side + 1:]
    o_ref[...] = jnp.dot(a_ref[...], w_ref[...], preferred_element_type=F32)
    _side_casts(side_src, side_dst)


def _proj(a, w, name, side=()):
    m, k = a.shape
    n = w.shape[1]
    tn = TN // 2 if side else TN
    n_i, n_j = m // TM, n // tn
    specs = [_side_cast_specs(sw, layer, n_i, n_j, by_rows=True) for sw, layer in side]
    outs = pl.pallas_call(
        functools.partial(_proj_kernel, len(side)),
        grid=(n_i, n_j),
        in_specs=[pl.BlockSpec((TM, k), lambda i, j: (i, 0)),
                  pl.BlockSpec((k, tn), lambda i, j: (0, j))] + [sp[0] for sp in specs],
        out_specs=[pl.BlockSpec((TM, tn), lambda i, j: (i, j))] + [sp[1] for sp in specs],
        out_shape=[jax.ShapeDtypeStruct((m, n), F32)] + [sp[2] for sp in specs],
        compiler_params=_params("arbitrary", "arbitrary"),
        name=name,
    )(a, w, *[sw for sw, _ in side])
    return outs[0], outs[1:]


def _even_proj_kernel(xn_ref, wu_ref, wv_ref, wx_ref, wc_ref, wb_ref,
                      u_ref, v_ref, z_ref, gb_ref):
    x = xn_ref[...]

    def dot(w_ref):
        return jnp.dot(x, w_ref[...], preferred_element_type=F32)

    u_ref[...] = jax.nn.gelu(dot(wu_ref), approximate=True)
    v_ref[...] = jax.nn.gelu(dot(wv_ref), approximate=True)
    z_ref[...] = dot(wc_ref) * dot(wx_ref)
    gb_ref[...] = dot(wb_ref)


def _even_proj(xn, w_in):
    m, d = xn.shape
    width = w_in.shape[1] // 5
    nt = width // TN_EVEN

    def wspec(group):
        return pl.BlockSpec((d, TN_EVEN), lambda i, j: (0, group * nt + j))

    out = pl.BlockSpec((TM, TN_EVEN), lambda i, j: (i, j))
    return pl.pallas_call(
        _even_proj_kernel,
        grid=(m // TM, nt),
        in_specs=[pl.BlockSpec((TM, d), lambda i, j: (i, 0)),
                  wspec(0), wspec(1), wspec(2), wspec(3), wspec(4)],
        out_specs=[out] * 4,
        out_shape=[jax.ShapeDtypeStruct((m, width), F32)] * 4,
        compiler_params=_params("parallel", "arbitrary"),
        name="even_proj",
    )(xn, w_in, w_in, w_in, w_in, w_in)


def _even_mix_kernel(u_ref, v_ref, z_ref, gb_ref, zh_ref, w_ref, b_ref, cw_ref, s_ref, ab_ref):
    c = pl.program_id(0)
    n_prompt = pl.num_programs(0) - 1
    hd = u_ref.shape[1] // A_HEADS
    half = u_ref.shape[1]

    v = v_ref[...].astype(BF16)
    for h in range(A_HEADS):
        cols = slice(h * hd, (h + 1) * hd)
        mixed = jnp.dot(w_ref[0, h], v[:, cols], preferred_element_type=F32)
        mixed = mixed + b_ref[0][:, h:h + 1]
        ab_ref[:, cols] = (u_ref[:, cols] * mixed).astype(BF16)

    z = z_ref[...]
    row = lax.broadcasted_iota(jnp.int32, (CHUNK, 1), 0)
    r1 = pltpu.roll(z, 1, 0)
    r2 = pltpu.roll(z, 2, 0)
    cw = cw_ref[...]

    def emit(zp1, zp2):
        conv = cw[0:1] * zp2 + cw[1:2] * zp1 + cw[2:3] * z
        ab_ref[:, half:] = (gb_ref[...] * conv).astype(BF16)

    @pl.when(c < n_prompt)
    def _():
        keep = c % SEQ_CHUNKS != 0
        h1 = jnp.where(keep, zh_ref[7:8, :], 0.0)
        h2 = jnp.where(keep, zh_ref[6:7, :], 0.0)
        emit(jnp.where(row >= 1, r1, h1),
             jnp.where(row >= 2, r2, jnp.where(row == 1, h1, h2)))

    @pl.when(c == n_prompt)
    def _():
        t = row % 4
        emit(jnp.where(t >= 1, r1, s_ref[0]), jnp.where(t >= 2, r2, s_ref[1]))


def _even_mix(u, v, z, gb, wmix, bmix, conv_w, state_rows):
    m, half = u.shape
    nchunks = m // CHUNK
    blk = pl.BlockSpec((CHUNK, half), lambda c: (c, 0))
    sel = lambda c: (c // (nchunks - 1), 0, 0, 0)
    return pl.pallas_call(
        _even_mix_kernel,
        grid=(nchunks,),
        in_specs=[blk, blk, blk, blk,
                  pl.BlockSpec((8, half), lambda c: (jnp.maximum(c * (CHUNK // 8) - 1, 0), 0)),
                  pl.BlockSpec((1, A_HEADS, CHUNK, CHUNK), sel),
                  pl.BlockSpec((1, CHUNK, A_HEADS), lambda c: (c // (nchunks - 1), 0, 0)),
                  pl.BlockSpec((3, half), lambda c: (0, 0)),
                  pl.BlockSpec((2, CHUNK, half), lambda c: (0, 0, 0))],
        out_specs=pl.BlockSpec((CHUNK, 2 * half), lambda c: (c, 0)),
        out_shape=jax.ShapeDtypeStruct((m, 2 * half), BF16),
        compiler_params=_params("arbitrary"),
        name="even_mix",
    )(u, v, z, gb, z, wmix, bmix, conv_w, state_rows)


def _attn_prompt_kernel(q_ref, kc_ref, vc_ref, kp_ref, vp_ref, bias_ref, sink_ref, os_ref, o_ref):
    i = pl.program_id(0)
    last = pl.num_programs(0) - 1

    @pl.when(i < last)
    def _():
        _attn_prompt_block(i, q_ref, kc_ref, vc_ref, kp_ref, vp_ref, bias_ref, sink_ref, o_ref)

    @pl.when(i == last)
    def _():
        o_ref[...] = os_ref[...]


def _attn_prompt_block(i, q_ref, kc_ref, vc_ref, kp_ref, vp_ref, bias_ref, sink_ref, o_ref):
    nh = ATTN_HEADS
    width, lanes = nh * HEAD_DIM, nh * CHUNK
    kk = jnp.concatenate([kp_ref[...], kc_ref[...]], axis=0).astype(BF16)
    vt = jnp.concatenate([vp_ref[...], vc_ref[...]], axis=0).T.astype(BF16)
    key = lax.broadcasted_iota(jnp.int32, (2 * CHUNK, lanes), 0)
    qry = lax.broadcasted_iota(jnp.int32, (2 * CHUNK, lanes), 1) % CHUNK
    dist = qry + CHUNK - key
    has_prev = i % SEQ_CHUNKS != 0
    valid = (dist >= 0) & (dist < CHUNK) & (has_prev | (key >= CHUNK))
    for b in range(N_KV_HEADS * GQA_GROUP // nh):
        g = (b * nh) // GQA_GROUP
        gc = slice(g * HEAD_DIM, (g + 1) * HEAD_DIM)
        qt = (q_ref[:, b * width:(b + 1) * width].T * (HEAD_DIM ** -0.5)).astype(BF16)
        qt = jnp.concatenate([qt[h * HEAD_DIM:(h + 1) * HEAD_DIM] for h in range(nh)], axis=1)
        s = jnp.dot(kk[:, gc], qt, preferred_element_type=F32) + bias_ref[b]
        s = jnp.where(valid, s, NEG_INF)
        sk = sink_ref[b]
        m = jnp.maximum(jnp.max(s, axis=0, keepdims=True), sk)
        e = jnp.exp(s - m)
        den = jnp.sum(e, axis=0, keepdims=True) + jnp.exp(sk - m)
        ot = jnp.dot(vt[gc, :], e.astype(BF16), preferred_element_type=F32) / den
        for pair in range(nh // 2):
            lo = 2 * pair * CHUNK
            o = jnp.concatenate([ot[:, lo:lo + CHUNK], ot[:, lo + CHUNK:lo + 2 * CHUNK]], axis=0).T
            c0 = b * width + pair * 2 * HEAD_DIM
            o_ref[:, c0:c0 + 2 * HEAD_DIM] = o.astype(BF16)


def _attn_prompt(qkv, o_sample, bias_t, sink_rows):
    m = qkv.shape[0]
    nb = m // CHUNK
    dq = N_KV_HEADS * GQA_GROUP * HEAD_DIM
    dkv = N_KV_HEADS * HEAD_DIM
    kcol, vcol = dq // dkv, dq // dkv + 1
    cur = lambda i: jnp.minimum(i, nb - 2)
    prev = lambda i: jnp.maximum(cur(i) - 1, 0)
    return pl.pallas_call(
        _attn_prompt_kernel,
        grid=(nb,),
        in_specs=[pl.BlockSpec((CHUNK, dq), lambda i: (cur(i), 0)),
                  pl.BlockSpec((CHUNK, dkv), lambda i: (cur(i), kcol)),
                  pl.BlockSpec((CHUNK, dkv), lambda i: (cur(i), vcol)),
                  pl.BlockSpec((CHUNK, dkv), lambda i: (prev(i), kcol)),
                  pl.BlockSpec((CHUNK, dkv), lambda i: (prev(i), vcol)),
                  pl.BlockSpec(bias_t.shape, lambda i: (0, 0, 0)),
                  pl.BlockSpec(sink_rows.shape, lambda i: (0, 0, 0)),
                  pl.BlockSpec(o_sample.shape, lambda i: (0, 0))],
        out_specs=pl.BlockSpec((CHUNK, dq), lambda i: (i, 0)),
        out_shape=jax.ShapeDtypeStruct((m, dq), BF16),
        compiler_params=_params("arbitrary"),
        name="attn_prompt",
    )(qkv, qkv, qkv, qkv, qkv, bias_t, sink_rows, o_sample)


def _attn_sample_kernel(q_ref, kn_ref, vn_ref, kc_ref, vc_ref, bc_ref, bn_ref, sink_ref, o_ref):
    kc = kc_ref[0].astype(BF16)
    vc = vc_ref[0].astype(BF16)
    kn = kn_ref[0].astype(BF16)
    vn = vn_ref[0].astype(BF16)
    rows = q_ref.shape[2]
    dec = rows // GQA_GROUP
    t_c = lax.broadcasted_iota(jnp.int32, (rows, CHUNK), 0) % dec
    j_c = lax.broadcasted_iota(jnp.int32, (rows, CHUNK), 1)
    valid_c = j_c > t_c
    t_n = lax.broadcasted_iota(jnp.int32, (rows, PAD_NEW_KEYS), 0) % dec
    j_n = lax.broadcasted_iota(jnp.int32, (rows, PAD_NEW_KEYS), 1)
    valid_n = j_n <= t_n
    nt = (((1,), (1,)), ((), ()))
    for g in range(N_KV_HEADS):
        gc = slice(g * HEAD_DIM, (g + 1) * HEAD_DIM)
        q = (q_ref[0, g] * (HEAD_DIM ** -0.5)).astype(BF16)
        sc = lax.dot_general(q, kc[:, gc], nt, preferred_element_type=F32) + bc_ref[g]
        sn = lax.dot_general(q, kn[:, gc], nt, preferred_element_type=F32) + bn_ref[g]
        sc = jnp.where(valid_c, sc, NEG_INF)
        sn = jnp.where(valid_n, sn, NEG_INF)
        sk = sink_ref[g]
        m = jnp.maximum(jnp.maximum(jnp.max(sc, axis=-1, keepdims=True),
                                    jnp.max(sn, axis=-1, keepdims=True)), sk)
        pc = jnp.exp(sc - m)
        pn = jnp.exp(sn - m)
        den = (jnp.sum(pc, axis=-1, keepdims=True) + jnp.sum(pn, axis=-1, keepdims=True)
               + jnp.exp(sk - m))
        o = (jnp.dot(pc.astype(BF16), vc[:, gc], preferred_element_type=F32)
             + jnp.dot(pn.astype(BF16), vn[:, gc], preferred_element_type=F32))
        o_ref[0, g] = o / den


def _attn_sample(q, kn, vn, kc, vc, bias_c, bias_n, sink_rows):
    nb, _, rows, _ = q.shape
    dkv = N_KV_HEADS * HEAD_DIM
    full = lambda a: pl.BlockSpec(a.shape, lambda b: (0,) * a.ndim)
    return pl.pallas_call(
        _attn_sample_kernel,
        grid=(nb,),
        in_specs=[pl.BlockSpec((1, N_KV_HEADS, rows, HEAD_DIM), lambda b: (b, 0, 0, 0)),
                  pl.BlockSpec((1, PAD_NEW_KEYS, dkv), lambda b: (b, 0, 0)),
                  pl.BlockSpec((1, PAD_NEW_KEYS, dkv), lambda b: (b, 0, 0)),
                  pl.BlockSpec((1, CHUNK, dkv), lambda b: (b, 0, 0)),
                  pl.BlockSpec((1, CHUNK, dkv), lambda b: (b, 0, 0)),
                  full(bias_c), full(bias_n), full(sink_rows)],
        out_specs=pl.BlockSpec((1, N_KV_HEADS, rows, HEAD_DIM), lambda b: (b, 0, 0, 0)),
        out_shape=jax.ShapeDtypeStruct(q.shape, F32),
        compiler_params=_params("parallel"),
        name="attn_sample",
    )(q, kn, vn, kc, vc, bias_c, bias_n, sink_rows)


def _ffn_kernel(n_side, xn_ref, wg_ref, wu_ref, wd_ref, *refs):
    side_src, y_ref, side_dst = refs[:n_side], refs[n_side], refs[n_side + 1:]
    @pl.when(pl.program_id(1) == 0)
    def _():
        y_ref[...] = jnp.zeros_like(y_ref)

    tm = xn_ref.shape[0]
    cut = -(-tm // (2 * 16)) * 16
    for rows in (pl.ds(0, cut), pl.ds(cut, tm - cut)):
        x = xn_ref[rows, :]
        gate = jnp.dot(x, wg_ref[...], preferred_element_type=F32)
        up = jnp.dot(x, wu_ref[...], preferred_element_type=F32)
        h = (jax.nn.silu(gate) * up).astype(BF16)
        y_ref[rows, :] += jnp.dot(h, wd_ref[...], preferred_element_type=F32)
    _side_casts(side_src, side_dst)


def _ffn(xn, wg, wu, wd, side=()):
    m, d = xn.shape
    dff = wg.shape[1]
    n_i, n_f = m // TM, dff // TF
    specs = [_side_cast_specs(w, layer, n_i, n_f) for w, layer in side]
    outs = pl.pallas_call(
        functools.partial(_ffn_kernel, len(side)),
        grid=(n_i, n_f),
        in_specs=[pl.BlockSpec((TM, d), lambda i, f: (i, 0)),
                  pl.BlockSpec((d, TF), lambda i, f: (0, f)),
                  pl.BlockSpec((d, TF), lambda i, f: (0, f)),
                  pl.BlockSpec((TF, d), lambda i, f: (f, 0))] + [s[0] for s in specs],
        out_specs=[pl.BlockSpec((TM, d), lambda i, f: (i, 0), pipeline_mode=pl.Buffered(1))]
        + [s[1] for s in specs],
        out_shape=[jax.ShapeDtypeStruct((m, d), F32)] + [s[2] for s in specs],
        compiler_params=_params("arbitrary", "arbitrary"),
        name="ffn",
    )(xn, wg, wu, wd, *[w for w, _ in side])
    return outs[0], outs[1:]


def _t5_bucket(dist):
    max_exact = N_BUCKETS // 2
    d = jnp.maximum(dist, max_exact).astype(F32)
    large = max_exact + (jnp.log(d / max_exact) / math.log(MAX_DISTANCE / max_exact)
                         * (N_BUCKETS - max_exact)).astype(jnp.int32)
    return jnp.where(dist < max_exact, dist, jnp.minimum(large, N_BUCKETS - 1))


def _bias_tables(rel_bias, dec_seq):
    n_heads = rel_bias.shape[1]
    assert n_heads == N_KV_HEADS * GQA_GROUP
    by_dist = rel_bias.astype(F32)[_t5_bucket(jnp.arange(CHUNK))].T
    span = 3 * CHUNK
    row = by_dist[:, jnp.clip(jnp.arange(span) - (CHUNK - 1), 0, CHUNK - 1)]
    flat = jnp.broadcast_to(row[:, None, :], (n_heads, 2 * CHUNK, span)).reshape(n_heads, -1)
    start = 2 * CHUNK - 1
    bias_t = flat[:, start:start + 2 * CHUNK * (span - 1)].reshape(n_heads, 2 * CHUNK, span - 1)
    bias_t = bias_t[:, :, :CHUNK]
    nb = n_heads // ATTN_HEADS
    bias_t = bias_t.reshape(nb, ATTN_HEADS, 2 * CHUNK, CHUNK).transpose(0, 2, 1, 3)
    bias_t = bias_t.reshape(nb, 2 * CHUNK, ATTN_HEADS * CHUNK)
    t = jnp.arange(dec_seq)[:, None]
    dist_c = jnp.clip(t + CHUNK - jnp.arange(CHUNK)[None, :], 0, CHUNK - 1)
    dist_n = jnp.clip(t - jnp.arange(PAD_NEW_KEYS)[None, :], 0, CHUNK - 1)
    rows = GQA_GROUP * dec_seq
    bias_c = by_dist[:, dist_c].reshape(N_KV_HEADS, rows, CHUNK)
    bias_n = by_dist[:, dist_n].reshape(N_KV_HEADS, rows, PAD_NEW_KEYS)
    return bias_t, bias_c, bias_n


def _mix_tables(w_s, b_s, dec_batch, dec_seq):
    assert dec_batch * dec_seq == CHUNK
    w_p = jnp.tril(w_s)
    small = jnp.tril(w_s[:, :dec_seq, :dec_seq])
    eye = jnp.eye(dec_batch, dtype=w_s.dtype)
    w_d = jnp.einsum("bc,hij->hbicj", eye, small).reshape(w_s.shape)
    b_p = b_s.T
    b_d = jnp.tile(b_s[:, :dec_seq].T, (dec_batch, 1))
    return jnp.stack([w_p, w_d]).astype(BF16), jnp.stack([b_p, b_d])


def kernel(x_prompt, x_sample, state_conv, cache_win_k, cache_win_v, norm_mix_pre, norm_mix_post, norm_ffn_pre, norm_ffn_post, w_in_even, w_out_even, sgu_w, sgu_b, conv_w, w_qkv_odd, w_o_odd, attn_sinks, rel_bias, ffn_w_gate, ffn_w_up, ffn_w_down):
    batch, seq, d = x_prompt.shape
    dec_batch, dec_seq, _ = x_sample.shape
    depth = norm_mix_pre.shape[0]
    n_p = batch * seq
    n_s = dec_batch * dec_seq
    dq = N_KV_HEADS * GQA_GROUP * HEAD_DIM
    dkv = N_KV_HEADS * HEAD_DIM
    assert seq == SEQ_CHUNKS * CHUNK and n_s == CHUNK and dec_seq <= PAD_NEW_KEYS

    x_p, x_s = x_prompt.reshape(n_p, d), x_sample.reshape(n_s, d)
    xn = _norm_first(x_p, x_s, norm_mix_pre[0])
    bias_t, bias_c, bias_n = _bias_tables(rel_bias, dec_seq)

    def layer_weights(layer):
        mixer = ([(w_in_even, layer // 2), (w_out_even, layer // 2)] if layer % 2 == 0
                 else [(w_qkv_odd, layer // 2), (w_o_odd, layer // 2)])
        return mixer + [(ffn_w_gate, layer), (ffn_w_up, layer), (ffn_w_down, layer)]

    w_bf = [_cast(w_in_even, 0), _cast(w_out_even, 0)]
    w_down0 = _cast(ffn_w_down, 0)

    conv_p, conv_s, chunk_v_s = [], [], []
    win_kp, win_vp, win_ks, win_vs = [], [], [], []
    for layer in range(depth):
        i = layer // 2
        w_mix_in, w_mix_out = w_bf[:2]
        if layer % 2 == 0:
            u, v, z, gb = _even_proj(xn, w_mix_in)
            wmix, bmix = _mix_tables(sgu_w[i], sgu_b[i], dec_batch, dec_seq)
            st = state_conv[i]
            zero = jnp.zeros_like(st[:, :1])
            s1 = jnp.concatenate([st[:, 1:2], zero, zero, zero], axis=1)
            s2 = jnp.concatenate([st[:, 0:1], st[:, 1:2], zero, zero], axis=1)
            state_rows = jnp.stack([s1.reshape(n_s, -1), s2.reshape(n_s, -1)])
            a = _even_mix(u, v, z, gb, wmix, bmix, conv_w[i], state_rows)
            y, cast_gu = _proj(a, w_mix_out, "even_out",
                               [(ffn_w_gate, 0), (ffn_w_up, 0)] if layer == 0 else ())
            if layer == 0:
                w_bf = w_bf[:2] + list(cast_gu) + [w_down0]
            zc = z.shape[1]
            conv_p.append(jnp.stack([z[(b + 1) * seq - 2:(b + 1) * seq] for b in range(batch)]))
            conv_s.append(z[n_p:].reshape(dec_batch, dec_seq, zc)[:, dec_seq - 2:])
            chunk_v_s.append(v[n_p:].reshape(dec_batch, dec_seq, A_HEADS, zc // A_HEADS))
        else:
            qkv, _ = _proj(xn, w_mix_in, "qkv")
            qs = qkv[n_p:, :dq].reshape(dec_batch, dec_seq, N_KV_HEADS, GQA_GROUP, HEAD_DIM)
            qs = qs.transpose(0, 2, 3, 1, 4).reshape(dec_batch, N_KV_HEADS, GQA_GROUP * dec_seq, HEAD_DIM)
            k_new = qkv[n_p:, dq:dq + dkv].reshape(dec_batch, dec_seq, dkv)
            v_new = qkv[n_p:, dq + dkv:].reshape(dec_batch, dec_seq, dkv)
            pad = ((0, 0), (0, PAD_NEW_KEYS - dec_seq), (0, 0))
            sink_rows = jnp.repeat(attn_sinks[i].astype(F32), dec_seq).reshape(N_KV_HEADS, GQA_GROUP * dec_seq, 1)
            o_s = _attn_sample(qs, jnp.pad(k_new, pad), jnp.pad(v_new, pad),
                               cache_win_k[i].reshape(dec_batch, -1, dkv),
                               cache_win_v[i].reshape(dec_batch, -1, dkv),
                               bias_c, bias_n, sink_rows)
            o_s = o_s.reshape(dec_batch, N_KV_HEADS, GQA_GROUP, dec_seq, HEAD_DIM)
            o_s = o_s.transpose(0, 3, 1, 2, 4).reshape(n_s, dq).astype(BF16)
            sink_lanes = jnp.repeat(attn_sinks[i].astype(F32), CHUNK).reshape(-1, 1, ATTN_HEADS * CHUNK)
            y, _ = _proj(_attn_prompt(qkv, o_s, bias_t, sink_lanes), w_mix_out, "attn_out")
            win = CHUNK
            tail = lambda c0: jnp.stack([qkv[(b + 1) * seq - win:(b + 1) * seq, c0:c0 + dkv]
                                         for b in range(batch)]).reshape(batch, win, N_KV_HEADS, HEAD_DIM)
            win_kp.append(tail(dq))
            win_vp.append(tail(dq + dkv))
            win_ks.append(k_new.reshape(dec_batch, dec_seq, N_KV_HEADS, HEAD_DIM))
            win_vs.append(v_new.reshape(dec_batch, dec_seq, N_KV_HEADS, HEAD_DIM))
        if layer == 0:
            x, xn = _resnorm_first(x_p, x_s, y, norm_mix_post[layer], norm_ffn_pre[layer])
        else:
            x, xn = _resnorm(x, y, norm_mix_post[layer], norm_ffn_pre[layer])
        w_gate, w_up, w_down = w_bf[2:]
        y, w_bf = _ffn(xn, w_gate, w_up, w_down, layer_weights(layer + 1) if layer + 1 < depth else ())
        if layer + 1 < depth:
            x, xn = _resnorm(x, y, norm_ffn_post[layer], norm_mix_pre[layer + 1])
    out_p, out_s = _res_last(x, y, norm_ffn_post[depth - 1], n_p)

    def new_window(cache, new_rows):
        keep = cache.shape[2] + dec_seq - CHUNK
        return jnp.concatenate([cache[:, :, keep:], jnp.stack(new_rows)], axis=2)

    return (out_p.reshape(batch, seq, d), out_s.reshape(dec_batch, dec_seq, d),
            jnp.stack(conv_p), jnp.stack(conv_s), jnp.stack(win_kp), jnp.stack(win_vp),
            new_window(cache_win_k, win_ks), new_window(cache_win_v, win_vs), jnp.stack(chunk_v_s))
```

```python
import functools
import math

import jax
import jax.numpy as jnp
from jax import lax
from jax.experimental import pallas as pl
from jax.experimental.pallas import tpu as pltpu

F32 = jnp.float32
BF16 = jnp.bfloat16

EPS = 1e-6
NEG_INF = -1e30
CHUNK = 128
A_HEADS = 8
N_KV_HEADS = 8
GQA_GROUP = 8
HEAD_DIM = 64
N_BUCKETS = 32
MAX_DISTANCE = 128
SEQ_CHUNKS = 16

LANES = 128
BF16_ROWS = 16

VMEM_LIMIT_BYTES = 56 * 1024 * 1024

TM = 1040
TR = 320
TN = 1024
TN_EVEN = 256
TF = 256
TC = 256
ATTN_HEADS = 8
SAMPLE_BATCH_PER_STEP = 4
PAD_NEW_KEYS = BF16_ROWS


def _params(*sem):
    return pltpu.CompilerParams(dimension_semantics=sem, vmem_limit_bytes=VMEM_LIMIT_BYTES)


def _rms(x, g):
    return x * lax.rsqrt(jnp.mean(x * x, axis=-1, keepdims=True) + EPS) * g


def _two_source_specs(d, n_blocks):
    return [pl.BlockSpec((CHUNK, d), lambda i: (jnp.minimum(i, n_blocks - 2), 0)),
            pl.BlockSpec((CHUNK, d), lambda i: (0, 0))]


def _on_source(body, xp_ref, xs_ref):
    i = pl.program_id(0)
    last = pl.num_programs(0) - 1

    @pl.when(i < last)
    def _():
        body(xp_ref[...])

    @pl.when(i == last)
    def _():
        body(xs_ref[...])


def _norm_first_kernel(xp_ref, xs_ref, g_ref, xn_ref):
    def body(x):
        xn_ref[...] = _rms(x, g_ref[...]).astype(BF16)

    _on_source(body, xp_ref, xs_ref)


def _norm_first(xp, xs, g):
    d = xp.shape[1]
    m = xp.shape[0] + xs.shape[0]
    nb = m // CHUNK
    row = pl.BlockSpec((CHUNK, d), lambda i: (i, 0))
    vec = pl.BlockSpec((1, d), lambda i: (0, 0))
    return pl.pallas_call(
        _norm_first_kernel, grid=(nb,), in_specs=_two_source_specs(d, nb) + [vec], out_specs=row,
        out_shape=jax.ShapeDtypeStruct((m, d), BF16),
        compiler_params=_params("arbitrary"), name="norm_first",
    )(xp, xs, g.reshape(1, d))


def _resnorm_first_kernel(xp_ref, xs_ref, y_ref, gp_ref, gn_ref, xo_ref, xn_ref):
    def body(x):
        xo = x + _rms(y_ref[...], gp_ref[...])
        xo_ref[...] = xo
        xn_ref[...] = _rms(xo, gn_ref[...]).astype(BF16)

    _on_source(body, xp_ref, xs_ref)


def _resnorm_first(xp, xs, y, g_post, g_next):
    m, d = y.shape
    nb = m // CHUNK
    row = pl.BlockSpec((CHUNK, d), lambda i: (i, 0))
    vec = pl.BlockSpec((1, d), lambda i: (0, 0))
    return pl.pallas_call(
        _resnorm_first_kernel, grid=(nb,), in_specs=_two_source_specs(d, nb) + [row, vec, vec],
        out_specs=[row, row],
        out_shape=[jax.ShapeDtypeStruct((m, d), F32), jax.ShapeDtypeStruct((m, d), BF16)],
        compiler_params=_params("arbitrary"), name="resnorm_first",
    )(xp, xs, y, g_post.reshape(1, d), g_next.reshape(1, d))


def _resnorm_kernel(x_ref, y_ref, gp_ref, gn_ref, xo_ref, xn_ref):
    xo = x_ref[...] + _rms(y_ref[...], gp_ref[...])
    xo_ref[...] = xo
    xn_ref[...] = _rms(xo, gn_ref[...]).astype(BF16)


def _resnorm(x, y, g_post, g_next):
    m, d = x.shape
    row = pl.BlockSpec((TR, d), lambda i: (i, 0))
    vec = pl.BlockSpec((1, d), lambda i: (0, 0))
    return pl.pallas_call(
        _resnorm_kernel, grid=(m // TR,), in_specs=[row, row, vec, vec],
        out_specs=[row, row],
        out_shape=[jax.ShapeDtypeStruct((m, d), F32), jax.ShapeDtypeStruct((m, d), BF16)],
        compiler_params=_params("parallel"), name="resnorm",
    )(x, y, g_post.reshape(1, d), g_next.reshape(1, d))


def _res_last_kernel(x_ref, y_ref, gp_ref, op_ref, os_ref):
    i = pl.program_id(0)
    last = pl.num_programs(0) - 1
    xo = x_ref[...] + _rms(y_ref[...], gp_ref[...])

    @pl.when(i < last)
    def _():
        op_ref[...] = xo

    @pl.when(i == last)
    def _():
        os_ref[...] = xo


def _res_last(x, y, g_post, n_prompt):
    m, d = x.shape
    nb = m // CHUNK
    row = pl.BlockSpec((CHUNK, d), lambda i: (i, 0))
    vec = pl.BlockSpec((1, d), lambda i: (0, 0))
    return pl.pallas_call(
        _res_last_kernel, grid=(nb,), in_specs=[row, row, vec],
        out_specs=[pl.BlockSpec((CHUNK, d), lambda i: (jnp.minimum(i, nb - 2), 0)),
                   pl.BlockSpec((CHUNK, d), lambda i: (0, 0))],
        out_shape=[jax.ShapeDtypeStruct((n_prompt, d), F32),
                   jax.ShapeDtypeStruct((m - n_prompt, d), F32)],
        compiler_params=_params("arbitrary"), name="res_last",
    )(x, y, g_post.reshape(1, d))


def _cast_kernel(w_ref, o_ref):
    o_ref[...] = w_ref[...].astype(BF16)


def _cast(w, layer):
    _, r, c = w.shape
    return pl.pallas_call(
        _cast_kernel,
        grid=(r // TC,),
        in_specs=[pl.BlockSpec((None, TC, c), lambda i: (layer, i, 0))],
        out_specs=pl.BlockSpec((TC, c), lambda i: (i, 0)),
        out_shape=jax.ShapeDtypeStruct((r, c), BF16),
        compiler_params=_params("parallel"),
        name="cast",
    )(w)


def _side_casts(src_refs, dst_refs):
    for src, dst in zip(src_refs, dst_refs):
        dst[...] = src[...].astype(BF16)


def _side_cast_specs(w, layer, n_i, n_f, by_rows=False):
    _, r, c = w.shape
    shape = jax.ShapeDtypeStruct((r, c), BF16)
    if by_rows:
        n_r = max(n for n in range(1, n_i * n_f + 1) if r % n == 0 and (r // n) % BF16_ROWS == 0)
        step = lambda i, f: jnp.minimum(i * n_f + f, n_r - 1)
        return (pl.BlockSpec((None, r // n_r, c), lambda i, f: (layer, step(i, f), 0)),
                pl.BlockSpec((r // n_r, c), lambda i, f: (step(i, f), 0)), shape)
    br = r // n_i
    assert br * n_i == r and br % BF16_ROWS == 0
    n_c = max(n for n in range(1, n_f + 1) if c % n == 0 and (c // n) % LANES == 0)
    bc = c // n_c
    col = lambda f: jnp.minimum(f, n_c - 1)
    return (pl.BlockSpec((None, br, bc), lambda i, f: (layer, i, col(f))),
            pl.BlockSpec((br, bc), lambda i, f: (i, col(f))), shape)


def _proj_kernel(n_side, a_ref, w_ref, *refs):
    side_src, o_ref, side_dst = refs[:n_side], refs[n_side], refs[n_side + 1:]
    o_ref[...] = jnp.dot(a_ref[...], w_ref[...], preferred_element_type=F32)
    _side_casts(side_src, side_dst)


def _proj(a, w, name, side=()):
    m, k = a.shape
    n = w.shape[1]
    tn = TN // 2 if side else TN
    n_i, n_j = m // TM, n // tn
    specs = [_side_cast_specs(sw, layer, n_i, n_j, by_rows=True) for sw, layer in side]
    outs = pl.pallas_call(
        functools.partial(_proj_kernel, len(side)),
        grid=(n_i, n_j),
        in_specs=[pl.BlockSpec((TM, k), lambda i, j: (i, 0)),
                  pl.BlockSpec((k, tn), lambda i, j: (0, j))] + [sp[0] for sp in specs],
        out_specs=[pl.BlockSpec((TM, tn), lambda i, j: (i, j))] + [sp[1] for sp in specs],
        out_shape=[jax.ShapeDtypeStruct((m, n), F32)] + [sp[2] for sp in specs],
        compiler_params=_params("arbitrary", "arbitrary"),
        name=name,
    )(a, w, *[sw for sw, _ in side])
    return outs[0], outs[1:]


def _even_proj_kernel(xn_ref, wu_ref, wv_ref, wx_ref, wc_ref, wb_ref,
                      u_ref, v_ref, z_ref, gb_ref):
    x = xn_ref[...]

    def dot(w_ref):
        return jnp.dot(x, w_ref[...], preferred_element_type=F32)

    u_ref[...] = jax.nn.gelu(dot(wu_ref), approximate=True)
    v_ref[...] = jax.nn.gelu(dot(wv_ref), approximate=True)
    z_ref[...] = dot(wc_ref) * dot(wx_ref)
    gb_ref[...] = dot(wb_ref)


def _even_proj(xn, w_in):
    m, d = xn.shape
    width = w_in.shape[1] // 5
    nt = width // TN_EVEN

    def wspec(group):
        return pl.BlockSpec((d, TN_EVEN), lambda i, j: (0, group * nt + j))

    out = pl.BlockSpec((TM, TN_EVEN), lambda i, j: (i, j))
    return pl.pallas_call(
        _even_proj_kernel,
        grid=(m // TM, nt),
        in_specs=[pl.BlockSpec((TM, d), lambda i, j: (i, 0)),
                  wspec(0), wspec(1), wspec(2), wspec(3), wspec(4)],
        out_specs=[out] * 4,
        out_shape=[jax.ShapeDtypeStruct((m, width), F32)] * 4,
        compiler_params=_params("parallel", "arbitrary"),
        name="even_proj",
    )(xn, w_in, w_in, w_in, w_in, w_in)


def _even_mix_kernel(u_ref, v_ref, z_ref, gb_ref, zh_ref, w_ref, b_ref, cw_ref, s_ref, ab_ref):
    c = pl.program_id(0)
    n_prompt = pl.num_programs(0) - 1
    hd = u_ref.shape[1] // A_HEADS
    half = u_ref.shape[1]

    v = v_ref[...].astype(BF16)
    for h in range(A_HEADS):
        cols = slice(h * hd, (h + 1) * hd)
        mixed = jnp.dot(w_ref[0, h], v[:, cols], preferred_element_type=F32)
        mixed = mixed + b_ref[0][:, h:h + 1]
        ab_ref[:, cols] = (u_ref[:, cols] * mixed).astype(BF16)

    z = z_ref[...]
    row = lax.broadcasted_iota(jnp.int32, (CHUNK, 1), 0)
    r1 = pltpu.roll(z, 1, 0)
    r2 = pltpu.roll(z, 2, 0)
    cw = cw_ref[...]

    def emit(zp1, zp2):
        conv = cw[0:1] * zp2 + cw[1:2] * zp1 + cw[2:3] * z
        ab_ref[:, half:] = (gb_ref[...] * conv).astype(BF16)

    @pl.when(c < n_prompt)
    def _():
        keep = c % SEQ_CHUNKS != 0
        h1 = jnp.where(keep, zh_ref[7:8, :], 0.0)
        h2 = jnp.where(keep, zh_ref[6:7, :], 0.0)
        emit(jnp.where(row >= 1, r1, h1),
             jnp.where(row >= 2, r2, jnp.where(row == 1, h1, h2)))

    @pl.when(c == n_prompt)
    def _():
        t = row % 4
        emit(jnp.where(t >= 1, r1, s_ref[0]), jnp.where(t >= 2, r2, s_ref[1]))


def _even_mix(u, v, z, gb, wmix, bmix, conv_w, state_rows):
    m, half = u.shape
    nchunks = m // CHUNK
    blk = pl.BlockSpec((CHUNK, half), lambda c: (c, 0))
    sel = lambda c: (c // (nchunks - 1), 0, 0, 0)
    return pl.pallas_call(
        _even_mix_kernel,
        grid=(nchunks,),
        in_specs=[blk, blk, blk, blk,
                  pl.BlockSpec((8, half), lambda c: (jnp.maximum(c * (CHUNK // 8) - 1, 0), 0)),
                  pl.BlockSpec((1, A_HEADS, CHUNK, CHUNK), sel),
                  pl.BlockSpec((1, CHUNK, A_HEADS), lambda c: (c // (nchunks - 1), 0, 0)),
                  pl.BlockSpec((3, half), lambda c: (0, 0)),
                  pl.BlockSpec((2, CHUNK, half), lambda c: (0, 0, 0))],
        out_specs=pl.BlockSpec((CHUNK, 2 * half), lambda c: (c, 0)),
        out_shape=jax.ShapeDtypeStruct((m, 2 * half), BF16),
        compiler_params=_params("arbitrary"),
        name="even_mix",
    )(u, v, z, gb, z, wmix, bmix, conv_w, state_rows)


def _attn_prompt_kernel(q_ref, kc_ref, vc_ref, kp_ref, vp_ref, bias_ref, sink_ref, os_ref, o_ref):
    i = pl.program_id(0)
    last = pl.num_programs(0) - 1

    @pl.when(i < last)
    def _():
        _attn_prompt_block(i, q_ref, kc_ref, vc_ref, kp_ref, vp_ref, bias_ref, sink_ref, o_ref)

    @pl.when(i == last)
    def _():
        o_ref[...] = os_ref[...]


def _attn_prompt_block(i, q_ref, kc_ref, vc_ref, kp_ref, vp_ref, bias_ref, sink_ref, o_ref):
    nh = ATTN_HEADS
    width, lanes = nh * HEAD_DIM, nh * CHUNK
    kk = jnp.concatenate([kp_ref[...], kc_ref[...]], axis=0).astype(BF16)
    vt = jnp.concatenate([vp_ref[...], vc_ref[...]], axis=0).T.astype(BF16)
    key = lax.broadcasted_iota(jnp.int32, (2 * CHUNK, lanes), 0)
    qry = lax.broadcasted_iota(jnp.int32, (2 * CHUNK, lanes), 1) % CHUNK
    dist = qry + CHUNK - key
    has_prev = i % SEQ_CHUNKS != 0
    valid = (dist >= 0) & (dist < CHUNK) & (has_prev | (key >= CHUNK))
    for b in range(N_KV_HEADS * GQA_GROUP // nh):
        g = (b * nh) // GQA_GROUP
        gc = slice(g * HEAD_DIM, (g + 1) * HEAD_DIM)
        qt = (q_ref[:, b * width:(b + 1) * width].T * (HEAD_DIM ** -0.5)).astype(BF16)
        qt = jnp.concatenate([qt[h * HEAD_DIM:(h + 1) * HEAD_DIM] for h in range(nh)], axis=1)
        s = jnp.dot(kk[:, gc], qt, preferred_element_type=F32) + bias_ref[b]
        s = jnp.where(valid, s, NEG_INF)
        sk = sink_ref[b]
        m = jnp.maximum(jnp.max(s, axis=0, keepdims=True), sk)
        e = jnp.exp(s - m)
        den = jnp.sum(e, axis=0, keepdims=True) + jnp.exp(sk - m)
        ot = jnp.dot(vt[gc, :], e.astype(BF16), preferred_element_type=F32) / den
        for pair in range(nh // 2):
            lo = 2 * pair * CHUNK
            o = jnp.concatenate([ot[:, lo:lo + CHUNK], ot[:, lo + CHUNK:lo + 2 * CHUNK]], axis=0).T
            c0 = b * width + pair * 2 * HEAD_DIM
            o_ref[:, c0:c0 + 2 * HEAD_DIM] = o.astype(BF16)


def _attn_prompt(qkv, o_sample, bias_t, sink_rows):
    m = qkv.shape[0]
    nb = m // CHUNK
    dq = N_KV_HEADS * GQA_GROUP * HEAD_DIM
    dkv = N_KV_HEADS * HEAD_DIM
    kcol, vcol = dq // dkv, dq // dkv + 1
    cur = lambda i: jnp.minimum(i, nb - 2)
    prev = lambda i: jnp.maximum(cur(i) - 1, 0)
    return pl.pallas_call(
        _attn_prompt_kernel,
        grid=(nb,),
        in_specs=[pl.BlockSpec((CHUNK, dq), lambda i: (cur(i), 0)),
                  pl.BlockSpec((CHUNK, dkv), lambda i: (cur(i), kcol)),
                  pl.BlockSpec((CHUNK, dkv), lambda i: (cur(i), vcol)),
                  pl.BlockSpec((CHUNK, dkv), lambda i: (prev(i), kcol)),
                  pl.BlockSpec((CHUNK, dkv), lambda i: (prev(i), vcol)),
                  pl.BlockSpec(bias_t.shape, lambda i: (0, 0, 0)),
                  pl.BlockSpec(sink_rows.shape, lambda i: (0, 0, 0)),
                  pl.BlockSpec(o_sample.shape, lambda i: (0, 0))],
        out_specs=pl.BlockSpec((CHUNK, dq), lambda i: (i, 0)),
        out_shape=jax.ShapeDtypeStruct((m, dq), BF16),
        compiler_params=_params("arbitrary"),
        name="attn_prompt",
    )(qkv, qkv, qkv, qkv, qkv, bias_t, sink_rows, o_sample)


def _attn_sample_kernel(q_ref, kn_ref, vn_ref, kc_ref, vc_ref, bc_ref, bn_ref, sink_ref, o_ref):
    for b in range(q_ref.shape[0]):
        _attn_sample_one(b, q_ref, kn_ref, vn_ref, kc_ref, vc_ref, bc_ref, bn_ref, sink_ref, o_ref)


def _attn_sample_one(b, q_ref, kn_ref, vn_ref, kc_ref, vc_ref, bc_ref, bn_ref, sink_ref, o_ref):
    kc = kc_ref[b].astype(BF16)
    vc = vc_ref[b].astype(BF16)
    kn = kn_ref[b].astype(BF16)
    vn = vn_ref[b].astype(BF16)
    rows = q_ref.shape[1]
    per_group = rows // N_KV_HEADS
    dec = per_group // GQA_GROUP
    t_c = lax.broadcasted_iota(jnp.int32, (rows, CHUNK), 0) % dec
    j_c = lax.broadcasted_iota(jnp.int32, (rows, CHUNK), 1)
    valid_c = j_c > t_c
    t_n = lax.broadcasted_iota(jnp.int32, (rows, PAD_NEW_KEYS), 0) % dec
    j_n = lax.broadcasted_iota(jnp.int32, (rows, PAD_NEW_KEYS), 1)
    valid_n = j_n <= t_n
    nt = (((1,), (1,)), ((), ()))
    q = (q_ref[b] * (HEAD_DIM ** -0.5)).astype(BF16)
    sc = lax.dot_general(q, kc, nt, preferred_element_type=F32) + bc_ref[...]
    sn = lax.dot_general(q, kn, nt, preferred_element_type=F32) + bn_ref[...]
    sc = jnp.where(valid_c, sc, NEG_INF)
    sn = jnp.where(valid_n, sn, NEG_INF)
    sk = sink_ref[...]
    m = jnp.maximum(jnp.maximum(jnp.max(sc, axis=-1, keepdims=True),
                                jnp.max(sn, axis=-1, keepdims=True)), sk)
    pc = jnp.exp(sc - m)
    pn = jnp.exp(sn - m)
    den = (jnp.sum(pc, axis=-1, keepdims=True) + jnp.sum(pn, axis=-1, keepdims=True)
           + jnp.exp(sk - m))
    o = (jnp.dot(pc.astype(BF16), vc, preferred_element_type=F32)
         + jnp.dot(pn.astype(BF16), vn, preferred_element_type=F32)) / den
    for g in range(N_KV_HEADS):
        o_ref[b, g] = o[g * per_group:(g + 1) * per_group, g * HEAD_DIM:(g + 1) * HEAD_DIM]


def _attn_sample(q, kn, vn, kc, vc, bias_c, bias_n, sink_rows):
    nb, rows, dkv = q.shape
    per = SAMPLE_BATCH_PER_STEP
    assert nb % per == 0
    full = lambda a: pl.BlockSpec(a.shape, lambda b: (0,) * a.ndim)
    return pl.pallas_call(
        _attn_sample_kernel,
        grid=(nb // per,),
        in_specs=[pl.BlockSpec((per, rows, dkv), lambda b: (b, 0, 0)),
                  pl.BlockSpec((per, PAD_NEW_KEYS, dkv), lambda b: (b, 0, 0)),
                  pl.BlockSpec((per, PAD_NEW_KEYS, dkv), lambda b: (b, 0, 0)),
                  pl.BlockSpec((per, CHUNK, dkv), lambda b: (b, 0, 0)),
                  pl.BlockSpec((per, CHUNK, dkv), lambda b: (b, 0, 0)),
                  full(bias_c), full(bias_n), full(sink_rows)],
        out_specs=pl.BlockSpec((per, N_KV_HEADS, rows // N_KV_HEADS, HEAD_DIM), lambda b: (b, 0, 0, 0)),
        out_shape=jax.ShapeDtypeStruct((nb, N_KV_HEADS, rows // N_KV_HEADS, HEAD_DIM), F32),
        compiler_params=_params("parallel"),
        name="attn_sample",
    )(q, kn, vn, kc, vc, bias_c, bias_n, sink_rows)


def _ffn_kernel(n_side, xn_ref, wg_ref, wu_ref, wd_ref, *refs):
    side_src, y_ref, side_dst = refs[:n_side], refs[n_side], refs[n_side + 1:]
    @pl.when(pl.program_id(1) == 0)
    def _():
        y_ref[...] = jnp.zeros_like(y_ref)

    tm = xn_ref.shape[0]
    cut = -(-tm // (2 * BF16_ROWS)) * BF16_ROWS
    for rows in (pl.ds(0, cut), pl.ds(cut, tm - cut)):
        x = xn_ref[rows, :]
        gate = jnp.dot(x, wg_ref[...], preferred_element_type=F32)
        up = jnp.dot(x, wu_ref[...], preferred_element_type=F32)
        h = (jax.nn.silu(gate) * up).astype(BF16)
        y_ref[rows, :] += jnp.dot(h, wd_ref[...], preferred_element_type=F32)
    _side_casts(side_src, side_dst)


def _ffn(xn, wg, wu, wd, side=()):
    m, d = xn.shape
    dff = wg.shape[1]
    n_i, n_f = m // TM, dff // TF
    specs = [_side_cast_specs(w, layer, n_i, n_f) for w, layer in side]
    outs = pl.pallas_call(
        functools.partial(_ffn_kernel, len(side)),
        grid=(n_i, n_f),
        in_specs=[pl.BlockSpec((TM, d), lambda i, f: (i, 0)),
                  pl.BlockSpec((d, TF), lambda i, f: (0, f)),
                  pl.BlockSpec((d, TF), lambda i, f: (0, f)),
                  pl.BlockSpec((TF, d), lambda i, f: (f, 0))] + [s[0] for s in specs],
        out_specs=[pl.BlockSpec((TM, d), lambda i, f: (i, 0), pipeline_mode=pl.Buffered(1))]
        + [s[1] for s in specs],
        out_shape=[jax.ShapeDtypeStruct((m, d), F32)] + [s[2] for s in specs],
        compiler_params=_params("arbitrary", "arbitrary"),
        name="ffn",
    )(xn, wg, wu, wd, *[w for w, _ in side])
    return outs[0], outs[1:]


def _t5_bucket(dist):
    max_exact = N_BUCKETS // 2
    d = jnp.maximum(dist, max_exact).astype(F32)
    large = max_exact + (jnp.log(d / max_exact) / math.log(MAX_DISTANCE / max_exact)
                         * (N_BUCKETS - max_exact)).astype(jnp.int32)
    return jnp.where(dist < max_exact, dist, jnp.minimum(large, N_BUCKETS - 1))


def _bias_tables(rel_bias, dec_seq):
    n_heads = rel_bias.shape[1]
    assert n_heads == N_KV_HEADS * GQA_GROUP
    by_dist = rel_bias.astype(F32)[_t5_bucket(jnp.arange(CHUNK))].T
    span = 3 * CHUNK
    row = by_dist[:, jnp.clip(jnp.arange(span) - (CHUNK - 1), 0, CHUNK - 1)]
    flat = jnp.broadcast_to(row[:, None, :], (n_heads, 2 * CHUNK, span)).reshape(n_heads, -1)
    start = 2 * CHUNK - 1
    bias_t = flat[:, start:start + 2 * CHUNK * (span - 1)].reshape(n_heads, 2 * CHUNK, span - 1)
    bias_t = bias_t[:, :, :CHUNK]
    nb = n_heads // ATTN_HEADS
    bias_t = bias_t.reshape(nb, ATTN_HEADS, 2 * CHUNK, CHUNK).transpose(0, 2, 1, 3)
    bias_t = bias_t.reshape(nb, 2 * CHUNK, ATTN_HEADS * CHUNK)
    t = jnp.arange(dec_seq)[:, None]
    dist_c = jnp.clip(t + CHUNK - jnp.arange(CHUNK)[None, :], 0, CHUNK - 1)
    dist_n = jnp.clip(t - jnp.arange(PAD_NEW_KEYS)[None, :], 0, CHUNK - 1)
    rows = GQA_GROUP * dec_seq
    bias_c = by_dist[:, dist_c].reshape(N_KV_HEADS * rows, CHUNK)
    bias_n = by_dist[:, dist_n].reshape(N_KV_HEADS * rows, PAD_NEW_KEYS)
    return bias_t, bias_c, bias_n


def _mix_tables(w_s, b_s, dec_batch, dec_seq):
    assert dec_batch * dec_seq == CHUNK
    w_p = jnp.tril(w_s)
    small = jnp.tril(w_s[:, :dec_seq, :dec_seq])
    eye = jnp.eye(dec_batch, dtype=w_s.dtype)
    w_d = jnp.einsum("bc,hij->hbicj", eye, small).reshape(w_s.shape)
    b_p = b_s.T
    b_d = jnp.tile(b_s[:, :dec_seq].T, (dec_batch, 1))
    return jnp.stack([w_p, w_d]).astype(BF16), jnp.stack([b_p, b_d])


def kernel(x_prompt, x_sample, state_conv, cache_win_k, cache_win_v, norm_mix_pre, norm_mix_post, norm_ffn_pre, norm_ffn_post, w_in_even, w_out_even, sgu_w, sgu_b, conv_w, w_qkv_odd, w_o_odd, attn_sinks, rel_bias, ffn_w_gate, ffn_w_up, ffn_w_down):
    batch, seq, d = x_prompt.shape
    dec_batch, dec_seq, _ = x_sample.shape
    depth = norm_mix_pre.shape[0]
    n_p = batch * seq
    n_s = dec_batch * dec_seq
    dq = N_KV_HEADS * GQA_GROUP * HEAD_DIM
    dkv = N_KV_HEADS * HEAD_DIM
    assert seq == SEQ_CHUNKS * CHUNK and n_s == CHUNK and dec_seq <= PAD_NEW_KEYS

    x_p, x_s = x_prompt.reshape(n_p, d), x_sample.reshape(n_s, d)
    xn = _norm_first(x_p, x_s, norm_mix_pre[0])
    bias_t, bias_c, bias_n = _bias_tables(rel_bias, dec_seq)

    def layer_weights(layer):
        mixer = ([(w_in_even, layer // 2), (w_out_even, layer // 2)] if layer % 2 == 0
                 else [(w_qkv_odd, layer // 2), (w_o_odd, layer // 2)])
        return mixer + [(ffn_w_gate, layer), (ffn_w_up, layer), (ffn_w_down, layer)]

    w_bf = [_cast(w_in_even, 0), _cast(w_out_even, 0)]
    w_down0 = _cast(ffn_w_down, 0)

    conv_p, conv_s, chunk_v_s = [], [], []
    win_kp, win_vp, win_ks, win_vs = [], [], [], []
    for layer in range(depth):
        i = layer // 2
        w_mix_in, w_mix_out = w_bf[:2]
        if layer % 2 == 0:
            u, v, z, gb = _even_proj(xn, w_mix_in)
            wmix, bmix = _mix_tables(sgu_w[i], sgu_b[i], dec_batch, dec_seq)
            st = state_conv[i]
            zero = jnp.zeros_like(st[:, :1])
            s1 = jnp.concatenate([st[:, 1:2], zero, zero, zero], axis=1)
            s2 = jnp.concatenate([st[:, 0:1], st[:, 1:2], zero, zero], axis=1)
            state_rows = jnp.stack([s1.reshape(n_s, -1), s2.reshape(n_s, -1)])
            a = _even_mix(u, v, z, gb, wmix, bmix, conv_w[i], state_rows)
            y, cast_gu = _proj(a, w_mix_out, "even_out",
                               [(ffn_w_gate, 0), (ffn_w_up, 0)] if layer == 0 else ())
            if layer == 0:
                w_bf = w_bf[:2] + list(cast_gu) + [w_down0]
            zc = z.shape[1]
            conv_p.append(jnp.stack([z[(b + 1) * seq - 2:(b + 1) * seq] for b in range(batch)]))
            conv_s.append(z[n_p:].reshape(dec_batch, dec_seq, zc)[:, dec_seq - 2:])
            chunk_v_s.append(v[n_p:].reshape(dec_batch, dec_seq, A_HEADS, zc // A_HEADS))
        else:
            qkv, _ = _proj(xn, w_mix_in, "qkv")
            qs = qkv[n_p:, :dq].reshape(dec_batch, dec_seq, N_KV_HEADS, GQA_GROUP, HEAD_DIM)
            qs = qs.transpose(0, 2, 3, 1, 4).reshape(dec_batch, N_KV_HEADS, GQA_GROUP * dec_seq, HEAD_DIM)
            qs = jnp.einsum("bgrd,gk->bgrkd", qs, jnp.eye(N_KV_HEADS, dtype=qs.dtype))
            qs = qs.reshape(dec_batch, N_KV_HEADS * GQA_GROUP * dec_seq, dkv)
            k_new = qkv[n_p:, dq:dq + dkv].reshape(dec_batch, dec_seq, dkv)
            v_new = qkv[n_p:, dq + dkv:].reshape(dec_batch, dec_seq, dkv)
            pad = ((0, 0), (0, PAD_NEW_KEYS - dec_seq), (0, 0))
            sink_rows = jnp.repeat(attn_sinks[i].astype(F32), dec_seq).reshape(-1, 1)
            o_s = _attn_sample(qs, jnp.pad(k_new, pad), jnp.pad(v_new, pad),
                               cache_win_k[i].reshape(dec_batch, -1, dkv),
                               cache_win_v[i].reshape(dec_batch, -1, dkv),
                               bias_c, bias_n, sink_rows)
            o_s = o_s.reshape(dec_batch, N_KV_HEADS, GQA_GROUP, dec_seq, HEAD_DIM)
            o_s = o_s.transpose(0, 3, 1, 2, 4).reshape(n_s, dq).astype(BF16)
            sink_lanes = jnp.repeat(attn_sinks[i].astype(F32), CHUNK).reshape(-1, 1, ATTN_HEADS * CHUNK)
            y, _ = _proj(_attn_prompt(qkv, o_s, bias_t, sink_lanes), w_mix_out, "attn_out")
            win = CHUNK
            tail = lambda c0: jnp.stack([qkv[(b + 1) * seq - win:(b + 1) * seq, c0:c0 + dkv]
                                         for b in range(batch)]).reshape(batch, win, N_KV_HEADS, HEAD_DIM)
            win_kp.append(tail(dq))
            win_vp.append(tail(dq + dkv))
            win_ks.append(k_new.reshape(dec_batch, dec_seq, N_KV_HEADS, HEAD_DIM))
            win_vs.append(v_new.reshape(dec_batch, dec_seq, N_KV_HEADS, HEAD_DIM))
        if layer == 0:
            x, xn = _resnorm_first(x_p, x_s, y, norm_mix_post[layer], norm_ffn_pre[layer])
        else:
            x, xn = _resnorm(x, y, norm_mix_post[layer], norm_ffn_pre[layer])
        w_gate, w_up, w_down = w_bf[2:]
        y, w_bf = _ffn(xn, w_gate, w_up, w_down, layer_weights(layer + 1) if layer + 1 < depth else ())
        if layer + 1 < depth:
            x, xn = _resnorm(x, y, norm_ffn_post[layer], norm_mix_pre[layer + 1])
    out_p, out_s = _res_last(x, y, norm_ffn_post[depth - 1], n_p)

    def new_window(cache, new_rows):
        keep = cache.shape[2] + dec_seq - CHUNK
        return jnp.concatenate([cache[:, :, keep:], jnp.stack(new_rows)], axis=2)

    return (out_p.reshape(batch, seq, d), out_s.reshape(dec_batch, dec_seq, d),
            jnp.stack(conv_p), jnp.stack(conv_s), jnp.stack(win_kp), jnp.stack(win_vp),
            new_window(cache_win_k, win_ks), new_window(cache_win_v, win_vs), jnp.stack(chunk_v_s))
```

```python
import functools
import math

import jax
import jax.numpy as jnp
from jax import lax
from jax.experimental import pallas as pl
from jax.experimental.pallas import tpu as pltpu

F32 = jnp.float32
BF16 = jnp.bfloat16

EPS = 1e-6
NEG_INF = -1e30
CHUNK = 128
A_HEADS = 8
N_KV_HEADS = 8
GQA_GROUP = 8
HEAD_DIM = 64
N_BUCKETS = 32
MAX_DISTANCE = 128
SEQ_CHUNKS = 16

LANES = 128
BF16_ROWS = 16

VMEM_LIMIT_BYTES = 56 * 1024 * 1024

TM = 1040
TR = 320
TN = 1024
TN_EVEN = 256
TF = 256
TC = 256
ATTN_HEADS = 8
SAMPLE_BATCH_PER_STEP = 4
PAD_NEW_KEYS = BF16_ROWS


def _params(*sem):
    return pltpu.CompilerParams(dimension_semantics=sem, vmem_limit_bytes=VMEM_LIMIT_BYTES)


def _rms(x, g):
    return x * lax.rsqrt(jnp.mean(x * x, axis=-1, keepdims=True) + EPS) * g


def _two_source_specs(d, n_blocks):
    return [pl.BlockSpec((CHUNK, d), lambda i: (jnp.minimum(i, n_blocks - 2), 0)),
            pl.BlockSpec((CHUNK, d), lambda i: (0, 0))]


def _on_source(body, xp_ref, xs_ref):
    i = pl.program_id(0)
    last = pl.num_programs(0) - 1

    @pl.when(i < last)
    def _():
        body(xp_ref[...])

    @pl.when(i == last)
    def _():
        body(xs_ref[...])


def _norm_first_kernel(xp_ref, xs_ref, g_ref, xn_ref):
    def body(x):
        xn_ref[...] = _rms(x, g_ref[...]).astype(BF16)

    _on_source(body, xp_ref, xs_ref)


def _norm_first(xp, xs, g):
    d = xp.shape[1]
    m = xp.shape[0] + xs.shape[0]
    nb = m // CHUNK
    row = pl.BlockSpec((CHUNK, d), lambda i: (i, 0))
    vec = pl.BlockSpec((1, d), lambda i: (0, 0))
    return pl.pallas_call(
        _norm_first_kernel, grid=(nb,), in_specs=_two_source_specs(d, nb) + [vec], out_specs=row,
        out_shape=jax.ShapeDtypeStruct((m, d), BF16),
        compiler_params=_params("arbitrary"), name="norm_first",
    )(xp, xs, g.reshape(1, d))


def _resnorm_first_kernel(xp_ref, xs_ref, y_ref, gp_ref, gn_ref, xo_ref, xn_ref):
    def body(x):
        xo = x + _rms(y_ref[...], gp_ref[...])
        xo_ref[...] = xo
        xn_ref[...] = _rms(xo, gn_ref[...]).astype(BF16)

    _on_source(body, xp_ref, xs_ref)


def _resnorm_first(xp, xs, y, g_post, g_next):
    m, d = y.shape
    nb = m // CHUNK
    row = pl.BlockSpec((CHUNK, d), lambda i: (i, 0))
    vec = pl.BlockSpec((1, d), lambda i: (0, 0))
    return pl.pallas_call(
        _resnorm_first_kernel, grid=(nb,), in_specs=_two_source_specs(d, nb) + [row, vec, vec],
        out_specs=[row, row],
        out_shape=[jax.ShapeDtypeStruct((m, d), F32), jax.ShapeDtypeStruct((m, d), BF16)],
        compiler_params=_params("arbitrary"), name="resnorm_first",
    )(xp, xs, y, g_post.reshape(1, d), g_next.reshape(1, d))


def _resnorm_kernel(x_ref, y_ref, gp_ref, gn_ref, xo_ref, xn_ref):
    xo = x_ref[...] + _rms(y_ref[...], gp_ref[...])
    xo_ref[...] = xo
    xn_ref[...] = _rms(xo, gn_ref[...]).astype(BF16)


def _resnorm(x, y, g_post, g_next):
    m, d = x.shape
    row = pl.BlockSpec((TR, d), lambda i: (i, 0))
    vec = pl.BlockSpec((1, d), lambda i: (0, 0))
    return pl.pallas_call(
        _resnorm_kernel, grid=(m // TR,), in_specs=[row, row, vec, vec],
        out_specs=[row, row],
        out_shape=[jax.ShapeDtypeStruct((m, d), F32), jax.ShapeDtypeStruct((m, d), BF16)],
        compiler_params=_params("parallel"), name="resnorm",
    )(x, y, g_post.reshape(1, d), g_next.reshape(1, d))


def _res_last_kernel(x_ref, y_ref, gp_ref, op_ref, os_ref):
    i = pl.program_id(0)
    last = pl.num_programs(0) - 1
    xo = x_ref[...] + _rms(y_ref[...], gp_ref[...])

    @pl.when(i < last)
    def _():
        op_ref[...] = xo

    @pl.when(i == last)
    def _():
        os_ref[...] = xo


def _res_last(x, y, g_post, n_prompt):
    m, d = x.shape
    nb = m // CHUNK
    row = pl.BlockSpec((CHUNK, d), lambda i: (i, 0))
    vec = pl.BlockSpec((1, d), lambda i: (0, 0))
    return pl.pallas_call(
        _res_last_kernel, grid=(nb,), in_specs=[row, row, vec],
        out_specs=[pl.BlockSpec((CHUNK, d), lambda i: (jnp.minimum(i, nb - 2), 0)),
                   pl.BlockSpec((CHUNK, d), lambda i: (0, 0))],
        out_shape=[jax.ShapeDtypeStruct((n_prompt, d), F32),
                   jax.ShapeDtypeStruct((m - n_prompt, d), F32)],
        compiler_params=_params("arbitrary"), name="res_last",
    )(x, y, g_post.reshape(1, d))


def _cast_kernel(w_ref, o_ref):
    o_ref[...] = w_ref[...].astype(BF16)


def _cast(w, layer):
    _, r, c = w.shape
    return pl.pallas_call(
        _cast_kernel,
        grid=(r // TC,),
        in_specs=[pl.BlockSpec((None, TC, c), lambda i: (layer, i, 0))],
        out_specs=pl.BlockSpec((TC, c), lambda i: (i, 0)),
        out_shape=jax.ShapeDtypeStruct((r, c), BF16),
        compiler_params=_params("parallel"),
        name="cast",
    )(w)


def _side_casts(src_refs, dst_refs):
    for src, dst in zip(src_refs, dst_refs):
        dst[...] = src[...].astype(BF16)


def _side_cast_specs(w, layer, n_i, n_f, by_rows=False):
    _, r, c = w.shape
    shape = jax.ShapeDtypeStruct((r, c), BF16)
    if by_rows:
        n_r = max(n for n in range(1, n_i * n_f + 1) if r % n == 0 and (r // n) % BF16_ROWS == 0)
        step = lambda i, f: jnp.minimum(i * n_f + f, n_r - 1)
        return (pl.BlockSpec((None, r // n_r, c), lambda i, f: (layer, step(i, f), 0)),
                pl.BlockSpec((r // n_r, c), lambda i, f: (step(i, f), 0)), shape)
    br = r // n_i
    assert br * n_i == r and br % BF16_ROWS == 0
    n_c = max(n for n in range(1, n_f + 1) if c % n == 0 and (c // n) % LANES == 0)
    bc = c // n_c
    col = lambda f: jnp.minimum(f, n_c - 1)
    return (pl.BlockSpec((None, br, bc), lambda i, f: (layer, i, col(f))),
            pl.BlockSpec((br, bc), lambda i, f: (i, col(f))), shape)


def _proj_kernel(n_side, a_ref, w_ref, *refs):
    side_src, o_ref, side_dst = refs[:n_side], refs[n_side], refs[n_side + 1:]
    o_ref[...] = jnp.dot(a_ref[...], w_ref[...], preferred_element_type=F32)
    _side_casts(side_src, side_dst)


def _proj(a, w, name, side=()):
    m, k = a.shape
    n = w.shape[1]
    tn = TN // 2 if side else TN
    n_i, n_j = m // TM, n // tn
    specs = [_side_cast_specs(sw, layer, n_i, n_j, by_rows=True) for sw, layer in side]
    outs = pl.pallas_call(
        functools.partial(_proj_kernel, len(side)),
        grid=(n_i, n_j),
        in_specs=[pl.BlockSpec((TM, k), lambda i, j: (i, 0)),
                  pl.BlockSpec((k, tn), lambda i, j: (0, j))] + [sp[0] for sp in specs],
        out_specs=[pl.BlockSpec((TM, tn), lambda i, j: (i, j))] + [sp[1] for sp in specs],
        out_shape=[jax.ShapeDtypeStruct((m, n), F32)] + [sp[2] for sp in specs],
        compiler_params=_params("arbitrary", "arbitrary"),
        name=name,
    )(a, w, *[sw for sw, _ in side])
    return outs[0], outs[1:]


def _even_proj_kernel(xn_ref, wu_ref, wv_ref, wx_ref, wc_ref, wb_ref,
                      u_ref, v_ref, z_ref, gb_ref):
    x = xn_ref[...]

    def dot(w_ref):
        return jnp.dot(x, w_ref[...], preferred_element_type=F32)

    u_ref[...] = jax.nn.gelu(dot(wu_ref), approximate=True)
    v_ref[...] = jax.nn.gelu(dot(wv_ref), approximate=True)
    z_ref[...] = dot(wc_ref) * dot(wx_ref)
    gb_ref[...] = dot(wb_ref)


def _even_proj(xn, w_in):
    m, d = xn.shape
    width = w_in.shape[1] // 5
    nt = width // TN_EVEN

    def wspec(group):
        return pl.BlockSpec((d, TN_EVEN), lambda i, j: (0, group * nt + j))

    out = pl.BlockSpec((TM, TN_EVEN), lambda i, j: (i, j))
    return pl.pallas_call(
        _even_proj_kernel,
        grid=(m // TM, nt),
        in_specs=[pl.BlockSpec((TM, d), lambda i, j: (i, 0)),
                  wspec(0), wspec(1), wspec(2), wspec(3), wspec(4)],
        out_specs=[out] * 4,
        out_shape=[jax.ShapeDtypeStruct((m, width), F32)] * 4,
        compiler_params=_params("parallel", "arbitrary"),
        name="even_proj",
    )(xn, w_in, w_in, w_in, w_in, w_in)


def _even_mix_kernel(u_ref, v_ref, z_ref, gb_ref, zh_ref, w_ref, b_ref, cw_ref, s_ref, ab_ref):
    c = pl.program_id(0)
    n_prompt = pl.num_programs(0) - 1
    hd = u_ref.shape[1] // A_HEADS
    half = u_ref.shape[1]

    v = v_ref[...].astype(BF16)
    for h in range(A_HEADS):
        cols = slice(h * hd, (h + 1) * hd)
        mixed = jnp.dot(w_ref[0, h], v[:, cols], preferred_element_type=F32)
        mixed = mixed + b_ref[0][:, h:h + 1]
        ab_ref[:, cols] = (u_ref[:, cols] * mixed).astype(BF16)

    z = z_ref[...]
    row = lax.broadcasted_iota(jnp.int32, (CHUNK, 1), 0)
    r1 = pltpu.roll(z, 1, 0)
    r2 = pltpu.roll(z, 2, 0)
    cw = cw_ref[...]

    def emit(zp1, zp2):
        conv = cw[0:1] * zp2 + cw[1:2] * zp1 + cw[2:3] * z
        ab_ref[:, half:] = (gb_ref[...] * conv).astype(BF16)

    @pl.when(c < n_prompt)
    def _():
        keep = c % SEQ_CHUNKS != 0
        h1 = jnp.where(keep, zh_ref[7:8, :], 0.0)
        h2 = jnp.where(keep, zh_ref[6:7, :], 0.0)
        emit(jnp.where(row >= 1, r1, h1),
             jnp.where(row >= 2, r2, jnp.where(row == 1, h1, h2)))

    @pl.when(c == n_prompt)
    def _():
        t = row % 4
        emit(jnp.where(t >= 1, r1, s_ref[0]), jnp.where(t >= 2, r2, s_ref[1]))


def _even_mix(u, v, z, gb, wmix, bmix, conv_w, state_rows):
    m, half = u.shape
    nchunks = m // CHUNK
    blk = pl.BlockSpec((CHUNK, half), lambda c: (c, 0))
    sel = lambda c: (c // (nchunks - 1), 0, 0, 0)
    return pl.pallas_call(
        _even_mix_kernel,
        grid=(nchunks,),
        in_specs=[blk, blk, blk, blk,
                  pl.BlockSpec((8, half), lambda c: (jnp.maximum(c * (CHUNK // 8) - 1, 0), 0)),
                  pl.BlockSpec((1, A_HEADS, CHUNK, CHUNK), sel),
                  pl.BlockSpec((1, CHUNK, A_HEADS), lambda c: (c // (nchunks - 1), 0, 0)),
                  pl.BlockSpec((3, half), lambda c: (0, 0)),
                  pl.BlockSpec((2, CHUNK, half), lambda c: (0, 0, 0))],
        out_specs=pl.BlockSpec((CHUNK, 2 * half), lambda c: (c, 0)),
        out_shape=jax.ShapeDtypeStruct((m, 2 * half), BF16),
        compiler_params=_params("arbitrary"),
        name="even_mix",
    )(u, v, z, gb, z, wmix, bmix, conv_w, state_rows)


def _attn_prompt_kernel(q_ref, kc_ref, vc_ref, kp_ref, vp_ref, bias_ref, sink_ref, os_ref, o_ref):
    i = pl.program_id(0)
    last = pl.num_programs(0) - 1

    @pl.when(i < last)
    def _():
        _attn_prompt_block(i, q_ref, kc_ref, vc_ref, kp_ref, vp_ref, bias_ref, sink_ref, o_ref)

    @pl.when(i == last)
    def _():
        o_ref[...] = os_ref[...]


def _attn_prompt_block(i, q_ref, kc_ref, vc_ref, kp_ref, vp_ref, bias_ref, sink_ref, o_ref):
    nh = ATTN_HEADS
    width, lanes = nh * HEAD_DIM, nh * CHUNK
    kk = jnp.concatenate([kp_ref[...], kc_ref[...]], axis=0).astype(BF16)
    vt = jnp.concatenate([vp_ref[...], vc_ref[...]], axis=0).T.astype(BF16)
    key = lax.broadcasted_iota(jnp.int32, (2 * CHUNK, lanes), 0)
    qry = lax.broadcasted_iota(jnp.int32, (2 * CHUNK, lanes), 1) % CHUNK
    dist = qry + CHUNK - key
    has_prev = i % SEQ_CHUNKS != 0
    valid = (dist >= 0) & (dist < CHUNK) & (has_prev | (key >= CHUNK))
    for b in range(N_KV_HEADS * GQA_GROUP // nh):
        g = (b * nh) // GQA_GROUP
        gc = slice(g * HEAD_DIM, (g + 1) * HEAD_DIM)
        qt = (q_ref[:, b * width:(b + 1) * width].T * (HEAD_DIM ** -0.5)).astype(BF16)
        qt = jnp.concatenate([qt[h * HEAD_DIM:(h + 1) * HEAD_DIM] for h in range(nh)], axis=1)
        s = jnp.dot(kk[:, gc], qt, preferred_element_type=F32) + bias_ref[b]
        s = jnp.where(valid, s, NEG_INF)
        sk = sink_ref[b]
        m = jnp.maximum(jnp.max(s, axis=0, keepdims=True), sk)
        e = jnp.exp(s - m)
        den = jnp.sum(e, axis=0, keepdims=True) + jnp.exp(sk - m)
        ot = jnp.dot(vt[gc, :], e.astype(BF16), preferred_element_type=F32) / den
        for pair in range(nh // 2):
            lo = 2 * pair * CHUNK
            o = jnp.concatenate([ot[:, lo:lo + CHUNK], ot[:, lo + CHUNK:lo + 2 * CHUNK]], axis=0).T
            c0 = b * width + pair * 2 * HEAD_DIM
            o_ref[:, c0:c0 + 2 * HEAD_DIM] = o.astype(BF16)


def _attn_prompt(qkv, o_sample, bias_t, sink_rows):
    m = qkv.shape[0]
    nb = m // CHUNK
    dq = N_KV_HEADS * GQA_GROUP * HEAD_DIM
    dkv = N_KV_HEADS * HEAD_DIM
    kcol, vcol = dq // dkv, dq // dkv + 1
    cur = lambda i: jnp.minimum(i, nb - 2)
    prev = lambda i: jnp.maximum(cur(i) - 1, 0)
    return pl.pallas_call(
        _attn_prompt_kernel,
        grid=(nb,),
        in_specs=[pl.BlockSpec((CHUNK, dq), lambda i: (cur(i), 0)),
                  pl.BlockSpec((CHUNK, dkv), lambda i: (cur(i), kcol)),
                  pl.BlockSpec((CHUNK, dkv), lambda i: (cur(i), vcol)),
                  pl.BlockSpec((CHUNK, dkv), lambda i: (prev(i), kcol)),
                  pl.BlockSpec((CHUNK, dkv), lambda i: (prev(i), vcol)),
                  pl.BlockSpec(bias_t.shape, lambda i: (0, 0, 0)),
                  pl.BlockSpec(sink_rows.shape, lambda i: (0, 0, 0)),
                  pl.BlockSpec(o_sample.shape, lambda i: (0, 0))],
        out_specs=pl.BlockSpec((CHUNK, dq), lambda i: (i, 0)),
        out_shape=jax.ShapeDtypeStruct((m, dq), BF16),
        compiler_params=_params("arbitrary"),
        name="attn_prompt",
    )(qkv, qkv, qkv, qkv, qkv, bias_t, sink_rows, o_sample)


def _attn_sample_kernel(q_ref, kn_ref, vn_ref, kc_ref, vc_ref, bc_ref, bn_ref, sink_ref, o_ref, qb_ref):
    qb_ref[...] = jnp.zeros_like(qb_ref)
    for b in range(q_ref.shape[0]):
        _attn_sample_one(b, q_ref, kn_ref, vn_ref, kc_ref, vc_ref, bc_ref, bn_ref, sink_ref, o_ref, qb_ref)


def _attn_sample_one(b, q_ref, kn_ref, vn_ref, kc_ref, vc_ref, bc_ref, bn_ref, sink_ref, o_ref, qb_ref):
    def window(ref):
        return jnp.concatenate([ref[b, :, g, :] for g in range(N_KV_HEADS)], axis=-1).astype(BF16)

    kc = window(kc_ref)
    vc = window(vc_ref)
    kn = kn_ref[b].astype(BF16)
    vn = vn_ref[b].astype(BF16)
    per_group = q_ref.shape[2]
    rows = per_group * N_KV_HEADS
    dec = per_group // GQA_GROUP
    for g in range(N_KV_HEADS):
        qb_ref[g * per_group:(g + 1) * per_group, g * HEAD_DIM:(g + 1) * HEAD_DIM] = q_ref[b, g]
    t_c = lax.broadcasted_iota(jnp.int32, (rows, CHUNK), 0) % dec
    j_c = lax.broadcasted_iota(jnp.int32, (rows, CHUNK), 1)
    valid_c = j_c > t_c
    t_n = lax.broadcasted_iota(jnp.int32, (rows, PAD_NEW_KEYS), 0) % dec
    j_n = lax.broadcasted_iota(jnp.int32, (rows, PAD_NEW_KEYS), 1)
    valid_n = j_n <= t_n
    nt = (((1,), (1,)), ((), ()))
    q = (qb_ref[...] * (HEAD_DIM ** -0.5)).astype(BF16)
    sc = lax.dot_general(q, kc, nt, preferred_element_type=F32) + bc_ref[...]
    sn = lax.dot_general(q, kn, nt, preferred_element_type=F32) + bn_ref[...]
    sc = jnp.where(valid_c, sc, NEG_INF)
    sn = jnp.where(valid_n, sn, NEG_INF)
    sk = sink_ref[...]
    m = jnp.maximum(jnp.maximum(jnp.max(sc, axis=-1, keepdims=True),
                                jnp.max(sn, axis=-1, keepdims=True)), sk)
    pc = jnp.exp(sc - m)
    pn = jnp.exp(sn - m)
    den = (jnp.sum(pc, axis=-1, keepdims=True) + jnp.sum(pn, axis=-1, keepdims=True)
           + jnp.exp(sk - m))
    o = (jnp.dot(pc.astype(BF16), vc, preferred_element_type=F32)
         + jnp.dot(pn.astype(BF16), vn, preferred_element_type=F32)) / den
    for g in range(N_KV_HEADS):
        o_ref[b, g] = o[g * per_group:(g + 1) * per_group, g * HEAD_DIM:(g + 1) * HEAD_DIM]


def _attn_sample(q, kn, vn, kc, vc, bias_c, bias_n, sink_rows):
    nb, _, per_group, _ = q.shape
    dkv = N_KV_HEADS * HEAD_DIM
    per = SAMPLE_BATCH_PER_STEP
    assert nb % per == 0
    full = lambda a: pl.BlockSpec(a.shape, lambda b: (0,) * a.ndim)
    return pl.pallas_call(
        _attn_sample_kernel,
        grid=(nb // per,),
        in_specs=[pl.BlockSpec((per, N_KV_HEADS, per_group, HEAD_DIM), lambda b: (b, 0, 0, 0)),
                  pl.BlockSpec((per, PAD_NEW_KEYS, dkv), lambda b: (b, 0, 0)),
                  pl.BlockSpec((per, PAD_NEW_KEYS, dkv), lambda b: (b, 0, 0)),
                  pl.BlockSpec((per, CHUNK, N_KV_HEADS, HEAD_DIM), lambda b: (b, 0, 0, 0)),
                  pl.BlockSpec((per, CHUNK, N_KV_HEADS, HEAD_DIM), lambda b: (b, 0, 0, 0)),
                  full(bias_c), full(bias_n), full(sink_rows)],
        out_specs=pl.BlockSpec((per, N_KV_HEADS, per_group, HEAD_DIM), lambda b: (b, 0, 0, 0)),
        out_shape=jax.ShapeDtypeStruct(q.shape, F32),
        scratch_shapes=[pltpu.VMEM((N_KV_HEADS * per_group, dkv), F32)],
        compiler_params=_params("parallel"),
        name="attn_sample",
    )(q, kn, vn, kc, vc, bias_c, bias_n, sink_rows)


def _ffn_kernel(n_side, xn_ref, wg_ref, wu_ref, wd_ref, *refs):
    side_src, y_ref, side_dst = refs[:n_side], refs[n_side], refs[n_side + 1:]
    @pl.when(pl.program_id(1) == 0)
    def _():
        y_ref[...] = jnp.zeros_like(y_ref)

    tm = xn_ref.shape[0]
    cut = -(-tm // (2 * BF16_ROWS)) * BF16_ROWS
    for rows in (pl.ds(0, cut), pl.ds(cut, tm - cut)):
        x = xn_ref[rows, :]
        gate = jnp.dot(x, wg_ref[...], preferred_element_type=F32)
        up = jnp.dot(x, wu_ref[...], preferred_element_type=F32)
        h = (jax.nn.silu(gate) * up).astype(BF16)
        y_ref[rows, :] += jnp.dot(h, wd_ref[...], preferred_element_type=F32)
    _side_casts(side_src, side_dst)


def _ffn(xn, wg, wu, wd, side=()):
    m, d = xn.shape
    dff = wg.shape[1]
    n_i, n_f = m // TM, dff // TF
    specs = [_side_cast_specs(w, layer, n_i, n_f) for w, layer in side]
    outs = pl.pallas_call(
        functools.partial(_ffn_kernel, len(side)),
        grid=(n_i, n_f),
        in_specs=[pl.BlockSpec((TM, d), lambda i, f: (i, 0)),
                  pl.BlockSpec((d, TF), lambda i, f: (0, f)),
                  pl.BlockSpec((d, TF), lambda i, f: (0, f)),
                  pl.BlockSpec((TF, d), lambda i, f: (f, 0))] + [s[0] for s in specs],
        out_specs=[pl.BlockSpec((TM, d), lambda i, f: (i, 0), pipeline_mode=pl.Buffered(1))]
        + [s[1] for s in specs],
        out_shape=[jax.ShapeDtypeStruct((m, d), F32)] + [s[2] for s in specs],
        compiler_params=_params("arbitrary", "arbitrary"),
        name="ffn",
    )(xn, wg, wu, wd, *[w for w, _ in side])
    return outs[0], outs[1:]


def _t5_bucket(dist):
    max_exact = N_BUCKETS // 2
    d = jnp.maximum(dist, max_exact).astype(F32)
    large = max_exact + (jnp.log(d / max_exact) / math.log(MAX_DISTANCE / max_exact)
                         * (N_BUCKETS - max_exact)).astype(jnp.int32)
    return jnp.where(dist < max_exact, dist, jnp.minimum(large, N_BUCKETS - 1))


def _bias_tables(rel_bias, dec_seq):
    n_heads = rel_bias.shape[1]
    assert n_heads == N_KV_HEADS * GQA_GROUP
    by_dist = rel_bias.astype(F32)[_t5_bucket(jnp.arange(CHUNK))].T
    span = 3 * CHUNK
    row = by_dist[:, jnp.clip(jnp.arange(span) - (CHUNK - 1), 0, CHUNK - 1)]
    nb = n_heads // ATTN_HEADS
    per = ATTN_HEADS * span
    flat = jnp.broadcast_to(row.reshape(nb, 1, per), (nb, 2 * CHUNK, per)).reshape(nb, -1)
    start = 2 * CHUNK - 1
    skew = flat[:, start:start + 2 * CHUNK * (per - 1)].reshape(nb, 2 * CHUNK, per - 1)
    bias_t = jnp.concatenate([skew[:, :, h * span:h * span + CHUNK] for h in range(ATTN_HEADS)], axis=-1)
    t = jnp.arange(dec_seq)[:, None]
    dist_c = jnp.clip(t + CHUNK - jnp.arange(CHUNK)[None, :], 0, CHUNK - 1)
    dist_n = jnp.clip(t - jnp.arange(PAD_NEW_KEYS)[None, :], 0, CHUNK - 1)
    rows = GQA_GROUP * dec_seq
    bias_c = by_dist[:, dist_c].reshape(N_KV_HEADS * rows, CHUNK)
    bias_n = by_dist[:, dist_n].reshape(N_KV_HEADS * rows, PAD_NEW_KEYS)
    return bias_t, bias_c, bias_n


def _mix_tables(w_s, b_s, dec_batch, dec_seq):
    assert dec_batch * dec_seq == CHUNK
    w_p = jnp.tril(w_s)
    small = jnp.tril(w_s[:, :dec_seq, :dec_seq])
    eye = jnp.eye(dec_batch, dtype=w_s.dtype)
    w_d = jnp.einsum("bc,hij->hbicj", eye, small).reshape(w_s.shape)
    b_p = b_s.T
    b_d = jnp.tile(b_s[:, :dec_seq].T, (dec_batch, 1))
    return jnp.stack([w_p, w_d]).astype(BF16), jnp.stack([b_p, b_d])


def kernel(x_prompt, x_sample, state_conv, cache_win_k, cache_win_v, norm_mix_pre, norm_mix_post, norm_ffn_pre, norm_ffn_post, w_in_even, w_out_even, sgu_w, sgu_b, conv_w, w_qkv_odd, w_o_odd, attn_sinks, rel_bias, ffn_w_gate, ffn_w_up, ffn_w_down):
    batch, seq, d = x_prompt.shape
    dec_batch, dec_seq, _ = x_sample.shape
    depth = norm_mix_pre.shape[0]
    n_p = batch * seq
    n_s = dec_batch * dec_seq
    dq = N_KV_HEADS * GQA_GROUP * HEAD_DIM
    dkv = N_KV_HEADS * HEAD_DIM
    assert seq == SEQ_CHUNKS * CHUNK and n_s == CHUNK and dec_seq <= PAD_NEW_KEYS

    x_p, x_s = x_prompt.reshape(n_p, d), x_sample.reshape(n_s, d)
    xn = _norm_first(x_p, x_s, norm_mix_pre[0])
    bias_t, bias_c, bias_n = _bias_tables(rel_bias, dec_seq)

    def layer_weights(layer):
        mixer = ([(w_in_even, layer // 2), (w_out_even, layer // 2)] if layer % 2 == 0
                 else [(w_qkv_odd, layer // 2), (w_o_odd, layer // 2)])
        return mixer + [(ffn_w_gate, layer), (ffn_w_up, layer), (ffn_w_down, layer)]

    w_bf = [_cast(w_in_even, 0), _cast(w_out_even, 0)]
    w_down0 = _cast(ffn_w_down, 0)

    conv_p, conv_s, chunk_v_s = [], [], []
    win_kp, win_vp, win_ks, win_vs = [], [], [], []
    for layer in range(depth):
        i = layer // 2
        w_mix_in, w_mix_out = w_bf[:2]
        if layer % 2 == 0:
            u, v, z, gb = _even_proj(xn, w_mix_in)
            wmix, bmix = _mix_tables(sgu_w[i], sgu_b[i], dec_batch, dec_seq)
            st = state_conv[i]
            zero = jnp.zeros_like(st[:, :1])
            s1 = jnp.concatenate([st[:, 1:2], zero, zero, zero], axis=1)
            s2 = jnp.concatenate([st[:, 0:1], st[:, 1:2], zero, zero], axis=1)
            state_rows = jnp.stack([s1.reshape(n_s, -1), s2.reshape(n_s, -1)])
            a = _even_mix(u, v, z, gb, wmix, bmix, conv_w[i], state_rows)
            y, cast_gu = _proj(a, w_mix_out, "even_out",
                               [(ffn_w_gate, 0), (ffn_w_up, 0)] if layer == 0 else ())
            if layer == 0:
                w_bf = w_bf[:2] + list(cast_gu) + [w_down0]
            zc = z.shape[1]
            conv_p.append(jnp.stack([z[(b + 1) * seq - 2:(b + 1) * seq] for b in range(batch)]))
            conv_s.append(z[n_p:].reshape(dec_batch, dec_seq, zc)[:, dec_seq - 2:])
            chunk_v_s.append(v[n_p:].reshape(dec_batch, dec_seq, A_HEADS, zc // A_HEADS))
        else:
            qkv, _ = _proj(xn, w_mix_in, "qkv")
            qs = qkv[n_p:, :dq].reshape(dec_batch, dec_seq, N_KV_HEADS, GQA_GROUP, HEAD_DIM)
            qs = qs.transpose(0, 2, 3, 1, 4).reshape(dec_batch, N_KV_HEADS, GQA_GROUP * dec_seq, HEAD_DIM)
            k_new = qkv[n_p:, dq:dq + dkv].reshape(dec_batch, dec_seq, dkv)
            v_new = qkv[n_p:, dq + dkv:].reshape(dec_batch, dec_seq, dkv)
            pad = ((0, 0), (0, PAD_NEW_KEYS - dec_seq), (0, 0))
            sink_rows = jnp.repeat(attn_sinks[i].astype(F32), dec_seq).reshape(-1, 1)
            o_s = _attn_sample(qs, jnp.pad(k_new, pad), jnp.pad(v_new, pad),
                               cache_win_k[i], cache_win_v[i],
                               bias_c, bias_n, sink_rows)
            o_s = o_s.reshape(dec_batch, N_KV_HEADS, GQA_GROUP, dec_seq, HEAD_DIM)
            o_s = o_s.transpose(0, 3, 1, 2, 4).reshape(n_s, dq).astype(BF16)
            sink_lanes = jnp.repeat(attn_sinks[i].astype(F32), CHUNK).reshape(-1, 1, ATTN_HEADS * CHUNK)
            y, _ = _proj(_attn_prompt(qkv, o_s, bias_t, sink_lanes), w_mix_out, "attn_out")
            win = CHUNK
            tail = lambda c0: jnp.stack([qkv[(b + 1) * seq - win:(b + 1) * seq, c0:c0 + dkv]
                                         for b in range(batch)]).reshape(batch, win, N_KV_HEADS, HEAD_DIM)
            win_kp.append(tail(dq))
            win_vp.append(tail(dq + dkv))
            win_ks.append(k_new.reshape(dec_batch, dec_seq, N_KV_HEADS, HEAD_DIM))
            win_vs.append(v_new.reshape(dec_batch, dec_seq, N_KV_HEADS, HEAD_DIM))
        if layer == 0:
            x, xn = _resnorm_first(x_p, x_s, y, norm_mix_post[layer], norm_ffn_pre[layer])
        else:
            x, xn = _resnorm(x, y, norm_mix_post[layer], norm_ffn_pre[layer])
        w_gate, w_up, w_down = w_bf[2:]
        y, w_bf = _ffn(xn, w_gate, w_up, w_down, layer_weights(layer + 1) if layer + 1 < depth else ())
        if layer + 1 < depth:
            x, xn = _resnorm(x, y, norm_ffn_post[layer], norm_mix_pre[layer + 1])
    out_p, out_s = _res_last(x, y, norm_ffn_post[depth - 1], n_p)

    def new_window(cache, new_rows):
        keep = cache.shape[2] + dec_seq - CHUNK
        return jnp.concatenate([cache[:, :, keep:], jnp.stack(new_rows)], axis=2)

    return (out_p.reshape(batch, seq, d), out_s.reshape(dec_batch, dec_seq, d),
            jnp.stack(conv_p), jnp.stack(conv_s), jnp.stack(win_kp), jnp.stack(win_vp),
            new_window(cache_win_k, win_ks), new_window(cache_win_v, win_vs), jnp.stack(chunk_v_s))
```
